```python
import math
import jax, jax.numpy as jnp
from jax import lax
import numpy as np

D_MODEL = 2048
BATCH = 2
SEQ = 4096
DEPTH = 4

HEAD_V = 128
N_HEADS_TOTAL = D_MODEL // HEAD_V
DIFF_HEADS = N_HEADS_TOTAL // 2
GLA_HEADS = N_HEADS_TOTAL // 4
RET_HEADS = N_HEADS_TOTAL - DIFF_HEADS - GLA_HEADS
DIFF_QK = HEAD_V // 2
GLA_DK = HEAD_V // 2
GLA_DV = HEAD_V
GLA_GATE_RANK = 16
GLA_TAU = 16.0
GLA_CHUNK = 32
RET_DK = HEAD_V // 2
RET_DV = HEAD_V
RET_CHUNK = 64
ROPE_BASE = 10000.0
N_BUCKETS = 32
MAX_DISTANCE = 128
Q_BLOCK = 128
D_FF = ((8 * D_MODEL // 3 + 255) // 256) * 256
CONV_W = 3
EPS = 1e-6

DIFF_W = DIFF_HEADS * HEAD_V
GLA_W = GLA_HEADS * GLA_DV
RET_W = RET_HEADS * RET_DV
MIX_W = DIFF_W + GLA_W + RET_W
IN_SPLITS = (DIFF_HEADS * 2 * DIFF_QK, DIFF_HEADS * 2 * DIFF_QK, DIFF_W,
             GLA_HEADS * GLA_DK, GLA_HEADS * GLA_DK, GLA_W, GLA_W, GLA_GATE_RANK, GLA_GATE_RANK,
             RET_HEADS * RET_DK, RET_HEADS * RET_DK, RET_W, RET_W)
IN_W = sum(IN_SPLITS)

kernel_name = 'hybrid_parallel_head_encoder'

F32 = jnp.float32


def rmsnorm(x, g):
    xf = x.astype(F32)
    y = xf * lax.rsqrt(jnp.mean(xf * xf, axis=-1, keepdims=True) + EPS)
    return (y * g.astype(F32)).astype(x.dtype)


def split_heads(t, n):
    b, s, _ = t.shape
    return t.reshape(b, s, n, -1).transpose(0, 2, 1, 3)


def head_rmsnorm(o, g):
    of = o.astype(F32)
    of = of * lax.rsqrt(jnp.mean(of * of, axis=-1, keepdims=True) + EPS)
    b, h, t, d = o.shape
    return (of.transpose(0, 2, 1, 3).reshape(b, t, h * d) * g.astype(F32)).astype(o.dtype)


def t5_bucket(rel):
    half = N_BUCKETS // 2
    max_exact = half // 2
    ret = jnp.where(rel > 0, half, 0)
    n = jnp.abs(rel)
    nf = jnp.maximum(n, 1).astype(F32)
    large = max_exact + (jnp.log(nf / max_exact) / math.log(MAX_DISTANCE / max_exact)
                         * (half - max_exact)).astype(jnp.int32)
    large = jnp.minimum(large, half - 1)
    return ret + jnp.where(n < max_exact, n, large)


def rotary(t, positions):
    half = t.shape[-1] // 2
    inv = 1.0 / (ROPE_BASE ** jnp.linspace(0.0, 1.0, half, dtype=F32))
    ang = positions.astype(F32)[:, None] * inv[None, :]
    cos, sin = jnp.cos(ang), jnp.sin(ang)
    t1, t2 = t[..., :half].astype(F32), t[..., half:].astype(F32)
    return jnp.concatenate([t1 * cos - t2 * sin, t1 * sin + t2 * cos], axis=-1).astype(t.dtype)


def diff_attention(q, k, v, lam, bias_rel):
    b, h, t, _ = q.shape
    nb = t // Q_BLOCK
    scale = DIFF_QK ** -0.5
    k1, k2 = k[..., :DIFF_QK], k[..., DIFF_QK:]
    q_blocks = q.reshape(b, h, nb, Q_BLOCK, -1).transpose(2, 0, 1, 3, 4)
    starts = jnp.arange(nb, dtype=jnp.int32) * Q_BLOCK
    key_pos = jnp.arange(t, dtype=jnp.int32)

    def block(args):
        qb, s = args
        q_pos = s + jnp.arange(Q_BLOCK, dtype=jnp.int32)
        bias = bias_rel[:, key_pos[None, :] - q_pos[:, None] + t - 1]
        s1 = jnp.einsum('bhqd,bhkd->bhqk', qb[..., :DIFF_QK], k1).astype(F32) * scale + bias
        s2 = jnp.einsum('bhqd,bhkd->bhqk', qb[..., DIFF_QK:], k2).astype(F32) * scale + bias
        p = jax.nn.softmax(s1, axis=-1) - lam * jax.nn.softmax(s2, axis=-1)
        return jnp.einsum('bhqk,bhkd->bhqd', p.astype(v.dtype), v)

    out = lax.map(block, (q_blocks, starts))
    return out.transpose(1, 2, 0, 3, 4).reshape(b, h, t, -1)


def chunk_states(k_end, v, chunk_decay):
    kv = jnp.einsum('bhnck,bhncv->nbhkv', k_end, v)
    dec = jnp.moveaxis(chunk_decay, 2, 0)[..., None]

    def step(state, inp):
        kv_n, d_n = inp
        return d_n * state + kv_n, state

    init = jnp.zeros(kv.shape[1:], kv.dtype)
    _, prev = lax.scan(step, init, (kv, dec))
    return jnp.moveaxis(prev, 0, 2)


def gla_direction(q, k, v, log_a):
    b, h, t, dk = q.shape
    dv = v.shape[-1]
    c = GLA_CHUNK
    n = t // c
    q, k = q.reshape(b, h, n, c, dk), k.reshape(b, h, n, c, dk)
    v = v.reshape(b, h, n, c, dv)
    cum = jnp.cumsum(log_a.astype(F32).reshape(b, h, n, c, dk), axis=3)
    last = cum[..., -1:, :]
    lower = jnp.tril(jnp.ones((c, c), dtype=bool))
    pair = jnp.where(lower[:, :, None],
                     jnp.exp(jnp.minimum(cum[..., :, None, :] - cum[..., None, :, :], 0.0)), 0.0)
    scores = jnp.einsum('bhnik,bhnjk,bhnijk->bhnij', q, k, pair.astype(q.dtype))
    intra = jnp.einsum('bhnij,bhnjv->bhniv', scores, v)
    states = chunk_states(k * jnp.exp(last - cum).astype(k.dtype), v,
                          jnp.exp(last[..., 0, :]).astype(k.dtype))
    inter = jnp.einsum('bhnik,bhnkv->bhniv', q * jnp.exp(cum).astype(q.dtype), states)
    return (intra + inter).reshape(b, h, t, dv)


def retention_direction(q, k, v, log_gamma):
    b, h, t, dk = q.shape
    dv = v.shape[-1]
    c = RET_CHUNK
    n = t // c
    q, k = q.reshape(b, h, n, c, dk), k.reshape(b, h, n, c, dk)
    v = v.reshape(b, h, n, c, dv)
    pos = jnp.arange(c, dtype=F32)
    rel = pos[:, None] - pos[None, :]
    lg = log_gamma.astype(F32)
    decay = jnp.where(rel >= 0, jnp.exp(lg[:, None, None] * jnp.maximum(rel, 0.0)), 0.0)
    scores = jnp.einsum('bhnid,bhnjd->bhnij', q, k) * decay[None, :, None].astype(q.dtype)
    intra = jnp.einsum('bhnij,bhnjv->bhniv', scores, v)
    xi = jnp.exp(lg[:, None] * (pos + 1.0))
    zeta = jnp.exp(lg[:, None] * (c - 1.0 - pos))
    chunk_dec = jnp.broadcast_to(jnp.exp(lg * c)[None, :, None, None], (b, h, n, dk))
    states = chunk_states(k * zeta[None, :, None, :, None].astype(k.dtype), v, chunk_dec.astype(k.dtype))
    inter = jnp.einsum('bhnid,bhndv->bhniv', q, states) * xi[None, :, None, :, None].astype(q.dtype)
    return (intra + inter).reshape(b, h, t, dv)


def flip_t(z):
    return jnp.flip(z, axis=2)


def hybrid_mixer(h, w_in, lam_params, lam_init, bias_rel, gate_w, gate_b,
                 decay_logit, head_gain, w_o):
    t = h.shape[1]
    proj = h @ w_in
    split_idx = [int(i) for i in np.cumsum(IN_SPLITS)[:-1]]
    (dq, dk, dv, gq, gk, gv, gr, ga_f, ga_b, rq, rk, rv, rg) = jnp.split(proj, split_idx, axis=-1)
    g_diff = head_gain[:DIFF_W]
    g_gla = head_gain[DIFF_W:DIFF_W + GLA_W]
    g_ret = head_gain[DIFF_W + GLA_W:]

    lp = lam_params.astype(F32)
    lam = jnp.exp(jnp.sum(lp[0] * lp[1])) - jnp.exp(jnp.sum(lp[2] * lp[3])) + lam_init
    a_out = diff_attention(split_heads(dq, DIFF_HEADS), split_heads(dk, DIFF_HEADS),
                           split_heads(dv, DIFF_HEADS), lam, bias_rel)
    a_out = head_rmsnorm(a_out, g_diff) * (1.0 - lam_init)

    q = split_heads(gq, GLA_HEADS) * (GLA_DK ** -0.5)
    k = split_heads(gk, GLA_HEADS)
    v = split_heads(gv, GLA_HEADS)
    la_f = split_heads(jax.nn.log_sigmoid((ga_f @ gate_w[0] + gate_b[0]).astype(F32)) / GLA_TAU, GLA_HEADS)
    la_b = split_heads(jax.nn.log_sigmoid((ga_b @ gate_w[1] + gate_b[1]).astype(F32)) / GLA_TAU, GLA_HEADS)
    o = gla_direction(q, k, v, la_f) + flip_t(gla_direction(flip_t(q), flip_t(k), flip_t(v), flip_t(la_b)))
    b_out = head_rmsnorm(o, g_gla) * jax.nn.silu(gr)

    pos = jnp.arange(t, dtype=jnp.int32)
    q = rotary(split_heads(rq, RET_HEADS), pos)
    k = rotary(split_heads(rk, RET_HEADS), pos) * (RET_DK ** -0.5)
    v = split_heads(rv, RET_HEADS)
    log_gamma = jax.nn.log_sigmoid(decay_logit.astype(F32))
    o = retention_direction(q, k, v, log_gamma[0]) + flip_t(
        retention_direction(flip_t(q), flip_t(k), flip_t(v), log_gamma[1]))
    c_out = head_rmsnorm(o, g_ret) * jax.nn.silu(rg)

    merged = jnp.concatenate([a_out, b_out, c_out], axis=-1).astype(h.dtype)
    return merged @ w_o


def conv_ffn(h, w_up, conv_w, conv_b, w_down):
    t = h.shape[1]
    u = h @ w_up
    pad = CONV_W // 2
    up = jnp.pad(u, ((0, 0), (pad, pad), (0, 0)))
    u = sum(up[:, i:i + t] * conv_w[i] for i in range(CONV_W)) + conv_b
    a, g = jnp.split(u, 2, axis=-1)
    return (a * jax.nn.silu(g)) @ w_down


def setup_inputs(seed: int = 0) -> dict:
    key = jax.random.key(seed)
    ks = jax.random.split(key, 16)
    nrm = jax.random.normal
    base_logit = jnp.log(2.0 ** (5.0 + jnp.arange(RET_HEADS, dtype=F32)) - 1.0)
    return {
        'x': nrm(ks[0], (BATCH, SEQ, D_MODEL), F32),
        'ln1_g': 1.0 + 0.02 * nrm(ks[1], (DEPTH, D_MODEL), F32),
        'w_in': nrm(ks[2], (DEPTH, D_MODEL, IN_W), F32) * D_MODEL ** -0.5,
        'diff_lambda': 0.1 * nrm(ks[3], (DEPTH, 4, DIFF_QK), F32),
        'rel_bias': 0.5 * nrm(ks[4], (N_BUCKETS, DIFF_HEADS), F32),
        'gla_gate_w': nrm(ks[5], (DEPTH, 2, GLA_GATE_RANK, GLA_HEADS * GLA_DK), F32) * GLA_GATE_RANK ** -0.5,
        'gla_gate_b': 0.1 * nrm(ks[6], (DEPTH, 2, GLA_HEADS * GLA_DK), F32),
        'ret_decay_logit': base_logit + 0.05 * nrm(ks[7], (DEPTH, 2, RET_HEADS), F32),
        'head_gain': 1.0 + 0.02 * nrm(ks[8], (DEPTH, MIX_W), F32),
        'w_o': nrm(ks[9], (DEPTH, MIX_W, D_MODEL), F32) * MIX_W ** -0.5,
        'ln2_g': 1.0 + 0.02 * nrm(ks[10], (DEPTH, D_MODEL), F32),
        'w_up': nrm(ks[11], (DEPTH, D_MODEL, 2 * D_FF), F32) * D_MODEL ** -0.5,
        'conv_w': nrm(ks[12], (DEPTH, CONV_W, 2 * D_FF), F32) * CONV_W ** -0.5,
        'conv_b': 0.02 * nrm(ks[13], (DEPTH, 2 * D_FF), F32),
        'w_down': nrm(ks[14], (DEPTH, D_FF, D_MODEL), F32) * D_FF ** -0.5,
        'final_g': 1.0 + 0.02 * nrm(ks[15], (D_MODEL,), F32),
    }


def reference(x, ln1_g, w_in, diff_lambda, rel_bias, gla_gate_w, gla_gate_b, ret_decay_logit,
              head_gain, w_o, ln2_g, w_up, conv_w, conv_b, w_down, final_g):
    t = x.shape[1]
    rel = jnp.arange(-(t - 1), t, dtype=jnp.int32)
    bias_rel = rel_bias.astype(F32)[t5_bucket(rel)].T
    for layer in range(DEPTH):
        lam_init = 0.8 - 0.6 * math.exp(-0.3 * layer)
        h = rmsnorm(x, ln1_g[layer])
        x = x + hybrid_mixer(h, w_in[layer], diff_lambda[layer], lam_init, bias_rel,
                             gla_gate_w[layer], gla_gate_b[layer], ret_decay_logit[layer],
                             head_gain[layer], w_o[layer])
        h = rmsnorm(x, ln2_g[layer])
        x = x + conv_ffn(h, w_up[layer], conv_w[layer], conv_b[layer], w_down[layer])
    return rmsnorm(x, final_g)
```

```python
import functools
import math

import jax
import jax.numpy as jnp
from jax import lax
from jax.experimental import pallas as pl
from jax.experimental.pallas import tpu as pltpu

F32 = jnp.float32
BF16 = jnp.bfloat16

HEAD_V = 128
DIFF_HEADS = 8
PAIR_HEADS = 2
N_PAIRS = 2
HEAD_K = 64
GLA_GATE_RANK = 16
GLA_TAU = 16.0
ROPE_BASE = 10000.0
N_BUCKETS = 32
MAX_DISTANCE = 128
CONV_W = 3
EPS = 1e-6

LANES = 128
SUBLANES = 8
BF16_ROWS = 16
VMEM_LIMIT_BYTES = 56 * 1024 * 1024

ATT_TILE = 256
GLA_CHUNK = 64
RET_CHUNK = 128
MM_TILE_M = 1024
MM_TILE_N = 1024
FFN_TILE_N = 512
HALO = SUBLANES

NEG_BIG = -1e30

_NT = (((1,), (1,)), ((), ()))
_TN = (((0,), (0,)), ((), ()))


def _params(sem):
    return pltpu.CompilerParams(dimension_semantics=sem, vmem_limit_bytes=VMEM_LIMIT_BYTES)


def _dot(a, b):
    return jnp.dot(a, b, preferred_element_type=F32)


def _dot_f32(a, b):
    return jnp.dot(a, b, preferred_element_type=F32, precision=lax.Precision.HIGHEST)


def _rms_rows(x, g):
    return x * lax.rsqrt(jnp.mean(x * x, axis=-1, keepdims=True) + EPS) * g


def _log_sigmoid(z):
    return jnp.minimum(z, 0.0) - jnp.log1p(jnp.exp(-jnp.abs(z)))


def _silu(z):
    return z * (1.0 / (1.0 + jnp.exp(-z)))


def _inproj_kernel(x_ref, g_ref, w_ref, wga_ref, proj_ref, ga_ref, h_scr):
    @pl.when(pl.program_id(1) == 0)
    def _():
        hb = _rms_rows(x_ref[...], g_ref[...]).astype(BF16)
        h_scr[...] = hb
        ga_ref[...] = _dot(hb, wga_ref[...])

    proj_ref[...] = _dot(h_scr[...], w_ref[...]).astype(BF16)


def _inproj(x2, g, w, wga):
    m, d = x2.shape
    n = w.shape[1]
    tm, tn = min(MM_TILE_M, m), min(MM_TILE_N, n)
    return pl.pallas_call(
        _inproj_kernel,
        grid=(m // tm, n // tn),
        in_specs=[
            pl.BlockSpec((tm, d), lambda i, j: (i, 0)),
            pl.BlockSpec((1, d), lambda i, j: (0, 0)),
            pl.BlockSpec((d, tn), lambda i, j: (0, j)),
            pl.BlockSpec((d, LANES), lambda i, j: (0, 0)),
        ],
        out_specs=[
            pl.BlockSpec((tm, tn), lambda i, j: (i, j)),
            pl.BlockSpec((tm, LANES), lambda i, j: (i, 0)),
        ],
        out_shape=[jax.ShapeDtypeStruct((m, n), BF16), jax.ShapeDtypeStruct((m, LANES), F32)],
        scratch_shapes=[pltpu.VMEM((tm, d), BF16)],
        compiler_params=_params(("parallel", "arbitrary")),
        name="inproj",
    )(x2, g, w, wga)


def _bias_tiles_kernel(rb_ref, bucket_ref, out_ref):
    h = pl.program_id(0)
    bucket = bucket_ref[...]
    acc = jnp.zeros(bucket.shape, F32)
    for b in range(N_BUCKETS):
        acc = jnp.where(bucket == b, rb_ref[b, h], acc)
    out_ref[...] = acc


def _bias_tiles(rel_bias, bucket):
    nd = bucket.shape[0]
    return pl.pallas_call(
        _bias_tiles_kernel,
        grid=(DIFF_HEADS,),
        in_specs=[
            pl.BlockSpec(memory_space=pltpu.SMEM),
            pl.BlockSpec((nd, ATT_TILE, ATT_TILE), lambda h: (0, 0, 0)),
        ],
        out_specs=pl.BlockSpec((None, nd, ATT_TILE, ATT_TILE), lambda h: (h, 0, 0, 0)),
        out_shape=jax.ShapeDtypeStruct((DIFF_HEADS, nd, ATT_TILE, ATT_TILE), F32),
        compiler_params=_params(("arbitrary",)),
        name="bias_tiles",
    )(rel_bias, bucket)


def _t5_bucket(rel):
    half = N_BUCKETS // 2
    max_exact = half // 2
    ret = jnp.where(rel > 0, half, 0)
    n = jnp.abs(rel)
    nf = jnp.maximum(n, 1).astype(F32)
    large = max_exact + (jnp.log(nf / max_exact) / math.log(MAX_DISTANCE / max_exact)
                         * (half - max_exact)).astype(jnp.int32)
    large = jnp.minimum(large, half - 1)
    return ret + jnp.where(n < max_exact, n, large)


ACC_ROWS = HEAD_V + BF16_ROWS


def _attn_kernel(q_ref, k_ref, v_ref, bias_ref, lam_ref, gain_ref, o_ref, vt_scr, *, lam_init, n_tiles):
    tile = ATT_TILE
    ones_row = (lax.broadcasted_iota(jnp.int32, (BF16_ROWS, tile), 0) == 0).astype(BF16)
    for c in range(n_tiles):
        vc = v_ref[c * tile:(c + 1) * tile, :].astype(F32)
        vt_scr[c, 0:HEAD_V, :] = vc.T.astype(BF16)
        vt_scr[c, HEAD_V:ACC_ROWS, :] = ones_row

    lp = lam_ref[...]
    lam = (jnp.exp(jnp.sum(lp[0:1] * lp[1:2], axis=-1, keepdims=True))
           - jnp.exp(jnp.sum(lp[2:3] * lp[3:4], axis=-1, keepdims=True)) + lam_init)
    first_half = lax.broadcasted_iota(jnp.int32, (1, 2 * HEAD_K), 1) < HEAD_K
    gain = gain_ref[...] * (1.0 - lam_init)

    def q_tile(qi, _):
        rows = pl.ds(pl.multiple_of(qi * tile, tile), tile)
        q = q_ref[rows, :].astype(F32) * (HEAD_K ** -0.5)
        qh = (jnp.where(first_half, q, 0.0).astype(BF16), jnp.where(first_half, 0.0, q).astype(BF16))

        def k_tile(c, carry):
            kc = k_ref[pl.ds(pl.multiple_of(c * tile, tile), tile), :]
            bias = bias_ref[jnp.clip(c - qi, -2, 2) + 2]
            vt = vt_scr[c]
            new = []
            for half in range(2):
                m, acc = carry[half]
                s = lax.dot_general(kc, qh[half], _NT, preferred_element_type=F32) + bias
                m_new = jnp.maximum(m, jnp.max(s, axis=0, keepdims=True))
                p = jnp.exp(s - m_new).astype(BF16)
                acc = acc * jnp.exp(m - m_new) + _dot(vt, p)
                new.append((m_new, acc))
            return tuple(new)

        init = tuple((jnp.full((1, tile), NEG_BIG, F32), jnp.zeros((ACC_ROWS, tile), F32)) for _ in range(2))
        (_, acc1), (_, acc2) = lax.fori_loop(0, n_tiles, k_tile, init)
        o1 = acc1[0:HEAD_V] / acc1[HEAD_V:HEAD_V + 1]
        o2 = acc2[0:HEAD_V] / acc2[HEAD_V:HEAD_V + 1]
        ot = o1 - lam * o2
        ot = ot * lax.rsqrt(jnp.mean(ot * ot, axis=0, keepdims=True) + EPS)
        o_ref[rows, :] = (ot.T * gain).astype(BF16)
        return 0

    lax.fori_loop(0, n_tiles, q_tile, 0)


def _diff_attention(proj, bias_tiles, lam_params, gain, lam_init, batch, t):
    n_tiles = t // ATT_TILE
    nd = bias_tiles.shape[1]
    return pl.pallas_call(
        functools.partial(_attn_kernel, lam_init=lam_init, n_tiles=n_tiles),
        grid=(batch, DIFF_HEADS),
        in_specs=[
            pl.BlockSpec((t, HEAD_V), lambda b, h: (b, h)),
            pl.BlockSpec((t, HEAD_V), lambda b, h: (b, DIFF_HEADS + h)),
            pl.BlockSpec((t, HEAD_V), lambda b, h: (b, 2 * DIFF_HEADS + h)),
            pl.BlockSpec((None, nd, ATT_TILE, ATT_TILE), lambda b, h: (h, 0, 0, 0)),
            pl.BlockSpec(lam_params.shape, lambda b, h: (0, 0)),
            pl.BlockSpec((1, HEAD_V), lambda b, h: (0, h)),
        ],
        out_specs=pl.BlockSpec((t, HEAD_V), lambda b, h: (b, h)),
        out_shape=jax.ShapeDtypeStruct((batch * t, DIFF_HEADS * HEAD_V), BF16),
        scratch_shapes=[pltpu.VMEM((n_tiles, ACC_ROWS, ATT_TILE), BF16)],
        compiler_params=_params(("parallel", "arbitrary")),
        name="diff_attn",
    )(proj, proj, proj, bias_tiles, lam_params, gain)


def _linattn_kernel(*refs, mode, chunk, n_chunks):
    if mode == "gla":
        (q_ref, k_ref, v_ref, gate_ref, gain_ref, ga_ref, wf_ref, wb_ref, gb_ref,
         o_ref, b_scr, e_scr, tot_scr, kvf_scr, rnext_scr) = refs
    else:
        (q_ref, k_ref, v_ref, gate_ref, gain_ref, cos_ref, sin_ref, logit_ref,
         o_ref, kx_scr, kvf_scr, rnext_scr) = refs
    c = chunk
    kw = PAIR_HEADS * HEAD_K
    vw = PAIR_HEADS * HEAD_V

    lane_k = lax.broadcasted_iota(jnp.int32, (1, kw), 1)
    head0_k = lane_k < HEAD_K
    st_row = lax.broadcasted_iota(jnp.int32, (vw, kw), 0)
    st_lane = lax.broadcasted_iota(jnp.int32, (vw, kw), 1)
    same_head = (st_row < HEAD_V) == (st_lane < HEAD_K)
    ri = lax.broadcasted_iota(jnp.int32, (c, c), 0)
    ci = lax.broadcasted_iota(jnp.int32, (c, c), 1)
    lower, upper = ri >= ci, ri <= ci

    def chunk_rows(n):
        return pl.ds(pl.multiple_of(n * c, c), c)

    if mode == "ret":
        pr = lax.broadcasted_iota(jnp.int32, (kw, kw), 0)
        pc = lax.broadcasted_iota(jnp.int32, (kw, kw), 1)
        swap = (pr == (pc ^ (HEAD_K // 2))).astype(BF16)

        def rotary(x_bf16, rows):
            xr = _dot(x_bf16, swap)
            return x_bf16.astype(F32) * cos_ref[rows, :] + xr * sin_ref[rows, :]

        lg = _log_sigmoid(logit_ref[...])
        lgf, lgb = lg[0:1], lg[1:2]
        pos = lax.broadcasted_iota(jnp.int32, (c, kw), 0).astype(F32)
        b_const, e_const = (pos + 1.0) * lgf, pos * lgb
        totf_const, totb_const = float(c) * lgf, float(c) * lgb
        rel = (ri - ci).astype(F32)
        dmask = []
        for h in range(PAIR_HEADS):
            lf = lgf[:, h * HEAD_K:h * HEAD_K + 1]
            lb = lgb[:, h * HEAD_K:h * HEAD_K + 1]
            dmask.append(jnp.where(lower, jnp.exp(lf * jnp.maximum(rel, 0.0)), 0.0)
                         + jnp.where(upper, jnp.exp(lb * jnp.maximum(-rel, 0.0)), 0.0))
    else:
        incl = lower.astype(F32)
        excl = (ri > ci).astype(F32)

    def phase1(i, r_state):
        n = n_chunks - 1 - i
        rows = chunk_rows(n)
        v = v_ref[rows, :]
        if mode == "gla":
            k = k_ref[rows, :].astype(F32)
            ga = ga_ref[rows, :]
            laf = _log_sigmoid(_dot_f32(ga, wf_ref[...]) + gb_ref[0:1, :]) * (1.0 / GLA_TAU)
            lab = _log_sigmoid(_dot_f32(ga, wb_ref[...]) + gb_ref[1:2, :]) * (1.0 / GLA_TAU)
            b = _dot_f32(incl, laf)
            e = _dot_f32(excl, lab)
            totf = b[c - 1:c, :]
            totb = e[c - 1:c, :] + lab[c - 1:c, :]
            b_scr[rows, :] = b
            e_scr[rows, :] = e
            tot_scr[n, 0:SUBLANES, :] = jnp.broadcast_to(totf, (SUBLANES, kw))
            tot_scr[n, SUBLANES:2 * SUBLANES, :] = jnp.broadcast_to(totb, (SUBLANES, kw))
        else:
            k = rotary(k_ref[rows, :], rows) * (HEAD_K ** -0.5)
            kx_scr[rows, :] = k
            b, e, totf, totb = b_const, e_const, totf_const, totb_const
        kf = (k * jnp.exp(totf - b)).astype(BF16)
        kb = (k * jnp.exp(e)).astype(BF16)
        kvf = lax.dot_general(v, kf, _TN, preferred_element_type=F32)
        kvb = lax.dot_general(v, kb, _TN, preferred_element_type=F32)
        kvf_scr[n] = jnp.where(same_head, kvf, 0.0)
        rnext_scr[n] = r_state.astype(BF16)
        return jnp.exp(totb) * r_state + jnp.where(same_head, kvb, 0.0)

    lax.fori_loop(0, n_chunks, phase1, jnp.zeros((vw, kw), F32))

    def phase2(n, s_state):
        rows = chunk_rows(n)
        v = v_ref[rows, :]
        if mode == "gla":
            q = q_ref[rows, :].astype(F32) * (HEAD_K ** -0.5)
            k = k_ref[rows, :].astype(F32)
            b, e = b_scr[rows, :], e_scr[rows, :]
            totf, totb = tot_scr[n, 0:1, :], tot_scr[n, SUBLANES:SUBLANES + 1, :]
        else:
            q = rotary(q_ref[rows, :], rows)
            k = kx_scr[rows, :]
            b, e, totf, totb = b_const, e_const, totf_const, totb_const

        qf = (q * jnp.exp(b)).astype(BF16)
        qb = (q * jnp.exp(totb - e)).astype(BF16)
        inter = (lax.dot_general(qf, s_state.astype(BF16), _NT, preferred_element_type=F32)
                 + lax.dot_general(qb, rnext_scr[n], _NT, preferred_element_type=F32))

        if mode == "gla":
            bm = b[c // 2 - 1:c // 2, :]
            em = e[c // 2:c // 2 + 1, :]
            qtf, ktf = q * jnp.exp(b - bm), (k * jnp.exp(bm - b)).astype(BF16)
            qtb, ktb = q * jnp.exp(em - e), (k * jnp.exp(e - em)).astype(BF16)
        else:
            kb16 = k.astype(BF16)
        outs = []
        for h in range(PAIR_HEADS):
            mh = head0_k if h == 0 else jnp.logical_not(head0_k)
            if mode == "gla":
                sf = lax.dot_general(jnp.where(mh, qtf, 0.0).astype(BF16), ktf, _NT, preferred_element_type=F32)
                sb = lax.dot_general(jnp.where(mh, qtb, 0.0).astype(BF16), ktb, _NT, preferred_element_type=F32)
                sc = jnp.where(lower, sf, 0.0) + jnp.where(upper, sb, 0.0)
            else:
                sc = lax.dot_general(jnp.where(mh, q, 0.0).astype(BF16), kb16, _NT,
                                     preferred_element_type=F32) * dmask[h]
            outs.append(_dot(sc.astype(BF16), v[:, h * HEAD_V:(h + 1) * HEAD_V]))

        gate = gate_ref[rows, :].astype(F32)
        for h in range(PAIR_HEADS):
            cols = slice(h * HEAD_V, (h + 1) * HEAD_V)
            o = outs[h] + inter[:, cols]
            o = _rms_rows(o, gain_ref[:, cols]) * _silu(gate[:, cols])
            o_ref[rows, cols] = o.astype(BF16)

        kvf = kvf_scr[n]
        return jnp.exp(totf) * s_state + kvf

    lax.fori_loop(0, n_chunks, phase2, jnp.zeros((vw, kw), F32))


def _linattn(mode, proj, col0, gain, extra, batch, t):
    chunk = GLA_CHUNK if mode == "gla" else RET_CHUNK
    n_chunks = t // chunk
    kw, vw = PAIR_HEADS * HEAD_K, PAIR_HEADS * HEAD_V
    v0 = (col0 + 2 * N_PAIRS) // 2
    g0 = v0 + N_PAIRS
    in_specs = [
        pl.BlockSpec((t, kw), lambda b, p: (b, col0 + p)),
        pl.BlockSpec((t, kw), lambda b, p: (b, col0 + N_PAIRS + p)),
        pl.BlockSpec((t, vw), lambda b, p: (b, v0 + p)),
        pl.BlockSpec((t, vw), lambda b, p: (b, g0 + p)),
        pl.BlockSpec((1, vw), lambda b, p: (0, p)),
    ]
    args = [proj, proj, proj, proj, gain]
    state_scratch = [pltpu.VMEM((n_chunks, vw, kw), F32), pltpu.VMEM((n_chunks, vw, kw), BF16)]
    if mode == "gla":
        ga, wf, wb, gb = extra
        in_specs += [
            pl.BlockSpec((t, LANES), lambda b, p: (b, 0)),
            pl.BlockSpec((LANES, kw), lambda b, p: (0, p)),
            pl.BlockSpec((LANES, kw), lambda b, p: (0, p)),
            pl.BlockSpec((2, kw), lambda b, p: (0, p)),
        ]
        args += [ga, wf, wb, gb]
        scratch = [pltpu.VMEM((t, kw), F32), pltpu.VMEM((t, kw), F32),
                   pltpu.VMEM((n_chunks, 2 * SUBLANES, kw), F32)] + state_scratch
    else:
        cos, sin, logit = extra
        in_specs += [
            pl.BlockSpec((t, kw), lambda b, p: (0, 0)),
            pl.BlockSpec((t, kw), lambda b, p: (0, 0)),
            pl.BlockSpec((2, kw), lambda b, p: (0, p)),
        ]
        args += [cos, sin, logit]
        scratch = [pltpu.VMEM((t, kw), F32)] + state_scratch
    return pl.pallas_call(
        functools.partial(_linattn_kernel, mode=mode, chunk=chunk, n_chunks=n_chunks),
        grid=(batch, N_PAIRS),
        in_specs=in_specs,
        out_specs=pl.BlockSpec((t, vw), lambda b, p: (b, p)),
        out_shape=jax.ShapeDtypeStruct((batch * t, N_PAIRS * vw), BF16),
        scratch_shapes=scratch,
        compiler_params=_params(("parallel", "arbitrary")),
        name="linattn_" + mode,
    )(*args)


def _outproj_kernel(x_ref, a_ref, b_ref, c_ref, wa_ref, wb_ref, wc_ref, o_ref):
    o_ref[...] = (x_ref[...] + _dot(a_ref[...], wa_ref[...]) + _dot(b_ref[...], wb_ref[...])
                  + _dot(c_ref[...], wc_ref[...]))


def _outproj(x2, a, b, c, w_o):
    m, d = x2.shape
    tm, tn = min(MM_TILE_M, m), min(MM_TILE_N, d)
    wa_rows, wb_rows = a.shape[1], b.shape[1]
    nb = wa_rows // wb_rows
    return pl.pallas_call(
        _outproj_kernel,
        grid=(m // tm, d // tn),
        in_specs=[
            pl.BlockSpec((tm, tn), lambda i, j: (i, j)),
            pl.BlockSpec((tm, wa_rows), lambda i, j: (i, 0)),
            pl.BlockSpec((tm, wb_rows), lambda i, j: (i, 0)),
            pl.BlockSpec((tm, wb_rows), lambda i, j: (i, 0)),
            pl.BlockSpec((wa_rows, tn), lambda i, j: (0, j)),
            pl.BlockSpec((wb_rows, tn), lambda i, j: (nb, j)),
            pl.BlockSpec((wb_rows, tn), lambda i, j: (nb + 1, j)),
        ],
        out_specs=pl.BlockSpec((tm, tn), lambda i, j: (i, j)),
        out_shape=jax.ShapeDtypeStruct((m, d), F32),
        compiler_params=_params(("parallel", "arbitrary")),
        name="outproj",
    )(x2, a, b, c, w_o, w_o, w_o)


def _ffn_up_kernel(x_ref, xp_ref, xn_ref, g_ref, wa_ref, wg_ref, cwa_ref, cwg_ref, cba_ref, cbg_ref,
                   o_ref, h_scr, *, tiles_per_seq):
    i = pl.program_id(0)
    tm = x_ref.shape[0]

    @pl.when(pl.program_id(1) == 0)
    def _():
        g = g_ref[...]
        keep_prev = (i % tiles_per_seq != 0).astype(F32)
        keep_next = ((i + 1) % tiles_per_seq != 0).astype(F32)
        h_scr[0:HALO, :] = (_rms_rows(xp_ref[...], g) * keep_prev).astype(BF16)
        h_scr[HALO:HALO + tm, :] = _rms_rows(x_ref[...], g).astype(BF16)
        h_scr[HALO + tm:, :] = (_rms_rows(xn_ref[...], g) * keep_next).astype(BF16)

    h = h_scr[...]
    n_ext = tm + 2 * HALO

    def conv(w_ref, cw_ref, cb_ref):
        u = _dot(h, w_ref[...])
        prev = pltpu.roll(u, 1, 0)[HALO:HALO + tm]
        nxt = pltpu.roll(u, n_ext - 1, 0)[HALO:HALO + tm]
        return prev * cw_ref[0:1, :] + u[HALO:HALO + tm] * cw_ref[1:2, :] + nxt * cw_ref[2:3, :] + cb_ref[...]

    a = conv(wa_ref, cwa_ref, cba_ref)
    gt = conv(wg_ref, cwg_ref, cbg_ref)
    o_ref[...] = (a * _silu(gt)).astype(BF16)


def _ffn_up(x2, g, w_up, conv_w, conv_b, t):
    m, d = x2.shape
    d_ff = w_up.shape[1] // 2
    tm = min(MM_TILE_M // 2, t)
    tn = FFN_TILE_N
    nj = d_ff // tn
    blocks_per_tile = tm // HALO
    last_block = m // HALO - 1
    return pl.pallas_call(
        functools.partial(_ffn_up_kernel, tiles_per_seq=t // tm),
        grid=(m // tm, nj),
        in_specs=[
            pl.BlockSpec((tm, d), lambda i, j: (i, 0)),
            pl.BlockSpec((HALO, d), lambda i, j: (jnp.maximum(i * blocks_per_tile - 1, 0), 0)),
            pl.BlockSpec((HALO, d), lambda i, j: (jnp.minimum((i + 1) * blocks_per_tile, last_block), 0)),
            pl.BlockSpec((1, d), lambda i, j: (0, 0)),
            pl.BlockSpec((d, tn), lambda i, j: (0, j)),
            pl.BlockSpec((d, tn), lambda i, j: (0, nj + j)),
            pl.BlockSpec((CONV_W, tn), lambda i, j: (0, j)),
            pl.BlockSpec((CONV_W, tn), lambda i, j: (0, nj + j)),
            pl.BlockSpec((1, tn), lambda i, j: (0, j)),
            pl.BlockSpec((1, tn), lambda i, j: (0, nj + j)),
        ],
        out_specs=pl.BlockSpec((tm, tn), lambda i, j: (i, j)),
        out_shape=jax.ShapeDtypeStruct((m, d_ff), BF16),
        scratch_shapes=[pltpu.VMEM((tm + 2 * HALO, d), BF16)],
        compiler_params=_params(("parallel", "arbitrary")),
        name="ffn_up",
    )(x2, x2, x2, g, w_up, w_up, conv_w, conv_w, conv_b, conv_b)


def _ffn_down_kernel(x_ref, a_ref, w_ref, o_ref):
    o_ref[...] = x_ref[...] + _dot(a_ref[...], w_ref[...])


def _ffn_down(x2, act, w_down):
    m, d = x2.shape
    k = act.shape[1]
    tm, tn = min(MM_TILE_M, m), FFN_TILE_N
    return pl.pallas_call(
        _ffn_down_kernel,
        grid=(m // tm, d // tn),
        in_specs=[
            pl.BlockSpec((tm, tn), lambda i, j: (i, j)),
            pl.BlockSpec((tm, k), lambda i, j: (i, 0)),
            pl.BlockSpec((k, tn), lambda i, j: (0, j)),
        ],
        out_specs=pl.BlockSpec((tm, tn), lambda i, j: (i, j)),
        out_shape=jax.ShapeDtypeStruct((m, d), F32),
        compiler_params=_params(("parallel", "arbitrary")),
        name="ffn_down",
    )(x2, act, w_down)


def _rmsnorm_kernel(x_ref, g_ref, o_ref):
    o_ref[...] = _rms_rows(x_ref[...], g_ref[...])


def _rmsnorm(x2, g):
    m, d = x2.shape
    tm = min(MM_TILE_M, m)
    return pl.pallas_call(
        _rmsnorm_kernel,
        grid=(m // tm,),
        in_specs=[pl.BlockSpec((tm, d), lambda i: (i, 0)), pl.BlockSpec((1, d), lambda i: (0, 0))],
        out_specs=pl.BlockSpec((tm, d), lambda i: (i, 0)),
        out_shape=jax.ShapeDtypeStruct((m, d), F32),
        compiler_params=_params(("parallel",)),
        name="final_rmsnorm",
    )(x2, g)


def _rotary_tables(t):
    half = HEAD_K // 2
    inv = 1.0 / (ROPE_BASE ** jnp.linspace(0.0, 1.0, half, dtype=F32))
    ang = jnp.arange(t, dtype=jnp.int32).astype(F32)[:, None] * inv[None, :]
    cos, sin = jnp.cos(ang), jnp.sin(ang)
    cos_l = jnp.tile(cos, (1, 2 * PAIR_HEADS))
    sin_l = jnp.tile(jnp.concatenate([-sin, sin], axis=-1), (1, PAIR_HEADS))
    return cos_l, sin_l


def kernel(x, ln1_g, w_in, diff_lambda, rel_bias, gla_gate_w, gla_gate_b, ret_decay_logit, head_gain,
           w_o, ln2_g, w_up, conv_w, conv_b, w_down, final_g):
    batch, t, d = x.shape
    depth = w_in.shape[0]
    diff_w = DIFF_HEADS * HEAD_V
    gla_w = 2 * N_PAIRS * HEAD_V
    gla_k = 2 * N_PAIRS * HEAD_K
    ga0 = 3 * diff_w + 2 * gla_k + 2 * gla_w
    ga1 = ga0 + 2 * GLA_GATE_RANK
    gla_col0 = 3 * diff_w // LANES
    ret_col0 = ga0 // LANES

    w_main = jnp.concatenate([w_in[:, :, :ga0], w_in[:, :, ga1:]], axis=-1).astype(BF16)
    w_ga = jnp.pad(w_in[:, :, ga0:ga1], ((0, 0), (0, 0), (0, LANES - 2 * GLA_GATE_RANK))).astype(BF16)
    w_o16, w_up16, w_down16 = w_o.astype(BF16), w_up.astype(BF16), w_down.astype(BF16)
    r = GLA_GATE_RANK
    gate_wf = jnp.pad(gla_gate_w[:, 0], ((0, 0), (0, LANES - r), (0, 0)))
    gate_wb = jnp.pad(gla_gate_w[:, 1], ((0, 0), (r, LANES - 2 * r), (0, 0)))
    ret_logit = jnp.repeat(ret_decay_logit, HEAD_K, axis=-1)

    rel = (jnp.arange(ATT_TILE, dtype=jnp.int32)[:, None] - jnp.arange(ATT_TILE, dtype=jnp.int32)[None, :])
    rel = rel[None] + ATT_TILE * (jnp.arange(5, dtype=jnp.int32) - 2)[:, None, None]
    bias_tiles = _bias_tiles(rel_bias.astype(F32), _t5_bucket(rel))
    cos_l, sin_l = _rotary_tables(t)

    x2 = x.reshape(batch * t, d)
    for layer in range(depth):
        lam_init = 0.8 - 0.6 * math.exp(-0.3 * layer)
        gain = head_gain[layer][None, :]
        proj, ga = _inproj(x2, ln1_g[layer][None, :], w_main[layer], w_ga[layer])
        a_out = _diff_attention(proj, bias_tiles, diff_lambda[layer], gain[:, :diff_w], lam_init, batch, t)
        b_out = _linattn("gla", proj, gla_col0, gain[:, diff_w:diff_w + gla_w],
                         (ga, gate_wf[layer], gate_wb[layer], gla_gate_b[layer]), batch, t)
        c_out = _linattn("ret", proj, ret_col0, gain[:, diff_w + gla_w:],
                         (cos_l, sin_l, ret_logit[layer]), batch, t)
        x2 = _outproj(x2, a_out, b_out, c_out, w_o16[layer])
        act = _ffn_up(x2, ln2_g[layer][None, :], w_up16[layer], conv_w[layer], conv_b[layer][None, :], t)
        x2 = _ffn_down(x2, act, w_down16[layer])
    return _rmsnorm(x2, final_g[None, :]).reshape(batch, t, d)
```

```python
import functools
import math

import jax
import jax.numpy as jnp
from jax import lax
from jax.experimental import pallas as pl
from jax.experimental.pallas import tpu as pltpu

F32 = jnp.float32
BF16 = jnp.bfloat16

HEAD_V = 128
DIFF_HEADS = 8
PAIR_HEADS = 2
N_PAIRS = 2
HEAD_K = 64
GLA_GATE_RANK = 16
GLA_TAU = 16.0
ROPE_BASE = 10000.0
N_BUCKETS = 32
MAX_DISTANCE = 128
CONV_W = 3
EPS = 1e-6
LOG2E = math.log2(math.e)

LANES = 128
SUBLANES = 8
BF16_ROWS = 16
VMEM_LIMIT_BYTES = 56 * 1024 * 1024

ATT_TILE = 256
GLA_CHUNK = 64
RET_CHUNK = 128
MM_TILE_M = 1024
MM_TILE_N = 1024
FFN_TILE_N = 512
HALO = SUBLANES

NEG_BIG = -1e30

_NT = (((1,), (1,)), ((), ()))
_TN = (((0,), (0,)), ((), ()))


def _params(sem):
    return pltpu.CompilerParams(dimension_semantics=sem, vmem_limit_bytes=VMEM_LIMIT_BYTES)


def _dot(a, b):
    return jnp.dot(a, b, preferred_element_type=F32)


def _dot_f32(a, b):
    return jnp.dot(a, b, preferred_element_type=F32, precision=lax.Precision.HIGHEST)


def _rms_rows(x, g):
    return x * lax.rsqrt(jnp.mean(x * x, axis=-1, keepdims=True) + EPS) * g


def _log_sigmoid(z):
    return jnp.minimum(z, 0.0) - jnp.log1p(jnp.exp(-jnp.abs(z)))


def _silu(z):
    return z * (1.0 / (1.0 + jnp.exp(-z)))


def _inproj_kernel(x_ref, g_ref, w_ref, wga_ref, cs_ref, proj_ref, ga_ref, h_scr):
    @pl.when(pl.program_id(1) == 0)
    def _():
        hb = _rms_rows(x_ref[...], g_ref[...]).astype(BF16)
        h_scr[...] = hb
        ga_ref[...] = _dot(hb, wga_ref[...])

    proj_ref[...] = (_dot(h_scr[...], w_ref[...]) * cs_ref[...]).astype(BF16)


def _inproj(x2, g, w, wga, colscale):
    m, d = x2.shape
    n = w.shape[1]
    tm, tn = min(MM_TILE_M, m), min(MM_TILE_N, n)
    return pl.pallas_call(
        _inproj_kernel,
        grid=(m // tm, n // tn),
        in_specs=[
            pl.BlockSpec((tm, d), lambda i, j: (i, 0)),
            pl.BlockSpec((1, d), lambda i, j: (0, 0)),
            pl.BlockSpec((d, tn), lambda i, j: (0, j)),
            pl.BlockSpec((d, LANES), lambda i, j: (0, 0)),
            pl.BlockSpec((1, tn), lambda i, j: (0, j)),
        ],
        out_specs=[
            pl.BlockSpec((tm, tn), lambda i, j: (i, j)),
            pl.BlockSpec((tm, LANES), lambda i, j: (i, 0)),
        ],
        out_shape=[jax.ShapeDtypeStruct((m, n), BF16), jax.ShapeDtypeStruct((m, LANES), F32)],
        scratch_shapes=[pltpu.VMEM((tm, d), BF16)],
        compiler_params=_params(("parallel", "arbitrary")),
        name="inproj",
    )(x2, g, w, wga, colscale)


def _bias_tiles_kernel(rb_ref, bucket_ref, out_ref):
    h = pl.program_id(0)
    bucket = bucket_ref[...]
    acc = jnp.zeros(bucket.shape, F32)
    for b in range(N_BUCKETS):
        acc = jnp.where(bucket == b, rb_ref[b, h], acc)
    out_ref[...] = acc * LOG2E


def _bias_tiles(rel_bias, bucket):
    nd = bucket.shape[0]
    return pl.pallas_call(
        _bias_tiles_kernel,
        grid=(DIFF_HEADS,),
        in_specs=[
            pl.BlockSpec(memory_space=pltpu.SMEM),
            pl.BlockSpec((nd, ATT_TILE, ATT_TILE), lambda h: (0, 0, 0)),
        ],
        out_specs=pl.BlockSpec((None, nd, ATT_TILE, ATT_TILE), lambda h: (h, 0, 0, 0)),
        out_shape=jax.ShapeDtypeStruct((DIFF_HEADS, nd, ATT_TILE, ATT_TILE), F32),
        compiler_params=_params(("arbitrary",)),
        name="bias_tiles",
    )(rel_bias, bucket)


def _t5_bucket(rel):
    half = N_BUCKETS // 2
    max_exact = half // 2
    ret = jnp.where(rel > 0, half, 0)
    n = jnp.abs(rel)
    nf = jnp.maximum(n, 1).astype(F32)
    large = max_exact + (jnp.log(nf / max_exact) / math.log(MAX_DISTANCE / max_exact)
                         * (half - max_exact)).astype(jnp.int32)
    large = jnp.minimum(large, half - 1)
    return ret + jnp.where(n < max_exact, n, large)


ACC_ROWS = HEAD_V + BF16_ROWS


def _attn_kernel(q_ref, k_ref, v_ref, bias_ref, lam_ref, gain_ref, o_ref,
                 vt_scr, qh_scr, s_scr, p_scr, mloc_scr, alpha_scr, m_scr, acc_scr, *, lam_init, n_tiles):
    tile = ATT_TILE
    ones_row = (lax.broadcasted_iota(jnp.int32, (BF16_ROWS, tile), 0) == 0).astype(BF16)
    for c in range(n_tiles):
        vc = v_ref[c * tile:(c + 1) * tile, :].astype(F32)
        vt_scr[c, 0:HEAD_V, :] = vc.T.astype(BF16)
        vt_scr[c, HEAD_V:ACC_ROWS, :] = ones_row

    lp = lam_ref[...]
    lam = (jnp.exp(jnp.sum(lp[0:1] * lp[1:2], axis=-1, keepdims=True))
           - jnp.exp(jnp.sum(lp[2:3] * lp[3:4], axis=-1, keepdims=True)) + lam_init)
    first_half = lax.broadcasted_iota(jnp.int32, (1, 2 * HEAD_K), 1) < HEAD_K
    gain = gain_ref[...] * (1.0 - lam_init)

    def q_tile(qi, _):
        rows = pl.ds(pl.multiple_of(qi * tile, tile), tile)
        q = q_ref[rows, :]
        zero = jnp.zeros_like(q)
        qh_scr[0] = jnp.where(first_half, q, zero)
        qh_scr[1] = jnp.where(first_half, zero, q)
        for half in range(2):
            m_scr[half] = jnp.full((1, tile), NEG_BIG, F32)
            acc_scr[half] = jnp.zeros((ACC_ROWS, tile), F32)

        def stage_a(c):
            kc = k_ref[pl.ds(pl.multiple_of(c * tile, tile), tile), :]
            bias = bias_ref[jnp.clip(c - qi, -2, 2) + 2]
            for half in range(2):
                s = lax.dot_general(kc, qh_scr[half], _NT, preferred_element_type=F32) + bias
                s_scr[half] = s
                mloc_scr[half] = jnp.max(s, axis=0, keepdims=True)

        def stage_b():
            for half in range(2):
                m_old = m_scr[half]
                m_new = jnp.maximum(m_old, mloc_scr[half])
                alpha_scr[half] = jnp.exp2(m_old - m_new)
                m_scr[half] = m_new
                p_scr[half] = jnp.exp2(s_scr[half] - m_new).astype(BF16)

        def stage_c(c):
            vt = vt_scr[c]
            for half in range(2):
                acc_scr[half] = acc_scr[half] * alpha_scr[half] + _dot(vt, p_scr[half])

        stage_a(0)
        stage_b()
        stage_a(1)

        def body(c, _):
            stage_c(c)
            stage_b()
            stage_a(c + 2)
            return 0

        lax.fori_loop(0, n_tiles - 2, body, 0)
        stage_c(n_tiles - 2)
        stage_b()
        stage_c(n_tiles - 1)

        acc1, acc2 = acc_scr[0], acc_scr[1]
        o1 = acc1[0:HEAD_V] / acc1[HEAD_V:HEAD_V + 1]
        o2 = acc2[0:HEAD_V] / acc2[HEAD_V:HEAD_V + 1]
        ot = o1 - lam * o2
        ot = ot * lax.rsqrt(jnp.mean(ot * ot, axis=0, keepdims=True) + EPS)
        o_ref[rows, :] = (ot.T * gain).astype(BF16)
        return 0

    lax.fori_loop(0, n_tiles, q_tile, 0)


def _diff_attention(proj, bias_tiles, lam_params, gain, lam_init, batch, t):
    n_tiles = t // ATT_TILE
    nd = bias_tiles.shape[1]
    tile = ATT_TILE
    return pl.pallas_call(
        functools.partial(_attn_kernel, lam_init=lam_init, n_tiles=n_tiles),
        grid=(batch, DIFF_HEADS),
        in_specs=[
            pl.BlockSpec((t, HEAD_V), lambda b, h: (b, h)),
            pl.BlockSpec((t, HEAD_V), lambda b, h: (b, DIFF_HEADS + h)),
            pl.BlockSpec((t, HEAD_V), lambda b, h: (b, 2 * DIFF_HEADS + h)),
            pl.BlockSpec((None, nd, tile, tile), lambda b, h: (h, 0, 0, 0)),
            pl.BlockSpec(lam_params.shape, lambda b, h: (0, 0)),
            pl.BlockSpec((1, HEAD_V), lambda b, h: (0, h)),
        ],
        out_specs=pl.BlockSpec((t, HEAD_V), lambda b, h: (b, h)),
        out_shape=jax.ShapeDtypeStruct((batch * t, DIFF_HEADS * HEAD_V), BF16),
        scratch_shapes=[
            pltpu.VMEM((n_tiles, ACC_ROWS, tile), BF16),
            pltpu.VMEM((2, tile, 2 * HEAD_K), BF16),
            pltpu.VMEM((2, tile, tile), F32),
            pltpu.VMEM((2, tile, tile), BF16),
            pltpu.VMEM((2, 1, tile), F32),
            pltpu.VMEM((2, 1, tile), F32),
            pltpu.VMEM((2, 1, tile), F32),
            pltpu.VMEM((2, ACC_ROWS, tile), F32),
        ],
        compiler_params=_params(("parallel", "arbitrary")),
        name="diff_attn",
    )(proj, proj, proj, bias_tiles, lam_params, gain)


def _linattn_kernel(*refs, mode, chunk, n_chunks):
    if mode == "gla":
        (q_ref, k_ref, v_ref, gate_ref, gain_ref, ga_ref, wf_ref, wb_ref, gb_ref,
         o_ref, b_scr, e_scr, tot_scr, kvf_scr, rnext_scr) = refs
    else:
        (q_ref, k_ref, v_ref, gate_ref, gain_ref, cos_ref, sin_ref, logit_ref,
         o_ref, kx_scr, kvf_scr, rnext_scr) = refs
    c = chunk
    kw = PAIR_HEADS * HEAD_K
    vw = PAIR_HEADS * HEAD_V

    lane_k = lax.broadcasted_iota(jnp.int32, (1, kw), 1)
    head0_k = lane_k < HEAD_K
    st_row = lax.broadcasted_iota(jnp.int32, (vw, kw), 0)
    st_lane = lax.broadcasted_iota(jnp.int32, (vw, kw), 1)
    same_head = (st_row < HEAD_V) == (st_lane < HEAD_K)
    ri = lax.broadcasted_iota(jnp.int32, (c, c), 0)
    ci = lax.broadcasted_iota(jnp.int32, (c, c), 1)
    lower, upper = ri >= ci, ri <= ci

    def chunk_rows(n):
        return pl.ds(pl.multiple_of(n * c, c), c)

    if mode == "ret":
        pr = lax.broadcasted_iota(jnp.int32, (kw, kw), 0)
        pc = lax.broadcasted_iota(jnp.int32, (kw, kw), 1)
        swap = (pr == (pc ^ (HEAD_K // 2))).astype(BF16)

        def rotary(x_bf16, rows):
            xr = _dot(x_bf16, swap)
            return x_bf16.astype(F32) * cos_ref[rows, :] + xr * sin_ref[rows, :]

        lg = _log_sigmoid(logit_ref[...])
        lgf, lgb = lg[0:1], lg[1:2]
        pos = lax.broadcasted_iota(jnp.int32, (c, kw), 0).astype(F32)
        b_const, e_const = (pos + 1.0) * lgf, pos * lgb
        totf_const, totb_const = float(c) * lgf, float(c) * lgb
        rel = (ri - ci).astype(F32)
        dmask = []
        for h in range(PAIR_HEADS):
            lf = lgf[:, h * HEAD_K:h * HEAD_K + 1]
            lb = lgb[:, h * HEAD_K:h * HEAD_K + 1]
            dmask.append(jnp.where(lower, jnp.exp(lf * jnp.maximum(rel, 0.0)), 0.0)
                         + jnp.where(upper, jnp.exp(lb * jnp.maximum(-rel, 0.0)), 0.0))
    else:
        incl = lower.astype(F32)
        excl = (ri > ci).astype(F32)

    def phase1(i, r_state):
        n = n_chunks - 1 - i
        rows = chunk_rows(n)
        v = v_ref[rows, :]
        if mode == "gla":
            k = k_ref[rows, :].astype(F32)
            ga = ga_ref[rows, :]
            laf = _log_sigmoid(_dot_f32(ga, wf_ref[...]) + gb_ref[0:1, :]) * (1.0 / GLA_TAU)
            lab = _log_sigmoid(_dot_f32(ga, wb_ref[...]) + gb_ref[1:2, :]) * (1.0 / GLA_TAU)
            b = _dot_f32(incl, laf)
            e = _dot_f32(excl, lab)
            totf = b[c - 1:c, :]
            totb = e[c - 1:c, :] + lab[c - 1:c, :]
            b_scr[rows, :] = b
            e_scr[rows, :] = e
            tot_scr[n, 0:SUBLANES, :] = jnp.broadcast_to(totf, (SUBLANES, kw))
            tot_scr[n, SUBLANES:2 * SUBLANES, :] = jnp.broadcast_to(totb, (SUBLANES, kw))
        else:
            k = rotary(k_ref[rows, :], rows)
            kx_scr[rows, :] = k
            b, e, totf, totb = b_const, e_const, totf_const, totb_const
        kf = (k * jnp.exp(totf - b)).astype(BF16)
        kb = (k * jnp.exp(e)).astype(BF16)
        kvf = lax.dot_general(v, kf, _TN, preferred_element_type=F32)
        kvb = lax.dot_general(v, kb, _TN, preferred_element_type=F32)
        kvf_scr[n] = jnp.where(same_head, kvf, 0.0)
        rnext_scr[n] = r_state.astype(BF16)
        return jnp.exp(totb) * r_state + jnp.where(same_head, kvb, 0.0)

    lax.fori_loop(0, n_chunks, phase1, jnp.zeros((vw, kw), F32))

    def phase2(n, s_state):
        rows = chunk_rows(n)
        v = v_ref[rows, :]
        if mode == "gla":
            q = q_ref[rows, :].astype(F32)
            k = k_ref[rows, :].astype(F32)
            b, e = b_scr[rows, :], e_scr[rows, :]
            totf, totb = tot_scr[n, 0:1, :], tot_scr[n, SUBLANES:SUBLANES + 1, :]
        else:
            q = rotary(q_ref[rows, :], rows)
            k = kx_scr[rows, :]
            b, e, totf, totb = b_const, e_const, totf_const, totb_const

        qf = (q * jnp.exp(b)).astype(BF16)
        qb = (q * jnp.exp(totb - e)).astype(BF16)
        inter = (lax.dot_general(qf, s_state.astype(BF16), _NT, preferred_element_type=F32)
                 + lax.dot_general(qb, rnext_scr[n], _NT, preferred_element_type=F32))

        if mode == "gla":
            bm = b[c // 2 - 1:c // 2, :]
            em = e[c // 2:c // 2 + 1, :]
            qtf, ktf = q * jnp.exp(b - bm), (k * jnp.exp(bm - b)).astype(BF16)
            qtb, ktb = q * jnp.exp(em - e), (k * jnp.exp(e - em)).astype(BF16)
        else:
            kb16 = k.astype(BF16)
        outs = []
        for h in range(PAIR_HEADS):
            mh = head0_k if h == 0 else jnp.logical_not(head0_k)
            if mode == "gla":
                sf = lax.dot_general(jnp.where(mh, qtf, 0.0).astype(BF16), ktf, _NT, preferred_element_type=F32)
                sb = lax.dot_general(jnp.where(mh, qtb, 0.0).astype(BF16), ktb, _NT, preferred_element_type=F32)
                sc = jnp.where(lower, sf, 0.0) + jnp.where(upper, sb, 0.0)
            else:
                sc = lax.dot_general(jnp.where(mh, q, 0.0).astype(BF16), kb16, _NT,
                                     preferred_element_type=F32) * dmask[h]
            outs.append(_dot(sc.astype(BF16), v[:, h * HEAD_V:(h + 1) * HEAD_V]))

        gate = gate_ref[rows, :].astype(F32)
        for h in range(PAIR_HEADS):
            cols = slice(h * HEAD_V, (h + 1) * HEAD_V)
            o = outs[h] + inter[:, cols]
            o = _rms_rows(o, gain_ref[:, cols]) * _silu(gate[:, cols])
            o_ref[rows, cols] = o.astype(BF16)

        kvf = kvf_scr[n]
        return jnp.exp(totf) * s_state + kvf

    lax.fori_loop(0, n_chunks, phase2, jnp.zeros((vw, kw), F32))


def _linattn(mode, proj, col0, gain, extra, batch, t):
    chunk = GLA_CHUNK if mode == "gla" else RET_CHUNK
    n_chunks = t // chunk
    kw, vw = PAIR_HEADS * HEAD_K, PAIR_HEADS * HEAD_V
    v0 = (col0 + 2 * N_PAIRS) // 2
    g0 = v0 + N_PAIRS
    in_specs = [
        pl.BlockSpec((t, kw), lambda b, p: (b, col0 + p)),
        pl.BlockSpec((t, kw), lambda b, p: (b, col0 + N_PAIRS + p)),
        pl.BlockSpec((t, vw), lambda b, p: (b, v0 + p)),
        pl.BlockSpec((t, vw), lambda b, p: (b, g0 + p)),
        pl.BlockSpec((1, vw), lambda b, p: (0, p)),
    ]
    args = [proj, proj, proj, proj, gain]
    state_scratch = [pltpu.VMEM((n_chunks, vw, kw), F32), pltpu.VMEM((n_chunks, vw, kw), BF16)]
    if mode == "gla":
        ga, wf, wb, gb = extra
        in_specs += [
            pl.BlockSpec((t, LANES), lambda b, p: (b, 0)),
            pl.BlockSpec((LANES, kw), lambda b, p: (0, p)),
            pl.BlockSpec((LANES, kw), lambda b, p: (0, p)),
            pl.BlockSpec((2, kw), lambda b, p: (0, p)),
        ]
        args += [ga, wf, wb, gb]
        scratch = [pltpu.VMEM((t, kw), F32), pltpu.VMEM((t, kw), F32),
                   pltpu.VMEM((n_chunks, 2 * SUBLANES, kw), F32)] + state_scratch
    else:
        cos, sin, logit = extra
        in_specs += [
            pl.BlockSpec((t, kw), lambda b, p: (0, 0)),
            pl.BlockSpec((t, kw), lambda b, p: (0, 0)),
            pl.BlockSpec((2, kw), lambda b, p: (0, p)),
        ]
        args += [cos, sin, logit]
        scratch = [pltpu.VMEM((t, kw), F32)] + state_scratch
    return pl.pallas_call(
        functools.partial(_linattn_kernel, mode=mode, chunk=chunk, n_chunks=n_chunks),
        grid=(batch, N_PAIRS),
        in_specs=in_specs,
        out_specs=pl.BlockSpec((t, vw), lambda b, p: (b, p)),
        out_shape=jax.ShapeDtypeStruct((batch * t, N_PAIRS * vw), BF16),
        scratch_shapes=scratch,
        compiler_params=_params(("parallel", "arbitrary")),
        name="linattn_" + mode,
    )(*args)


def _outproj_kernel(x_ref, a_ref, b_ref, c_ref, wa_ref, wb_ref, wc_ref, o_ref):
    o_ref[...] = (x_ref[...] + _dot(a_ref[...], wa_ref[...]) + _dot(b_ref[...], wb_ref[...])
                  + _dot(c_ref[...], wc_ref[...]))


def _outproj(x2, a, b, c, w_o):
    m, d = x2.shape
    tm, tn = min(MM_TILE_M, m), min(MM_TILE_N, d)
    wa_rows, wb_rows = a.shape[1], b.shape[1]
    nb = wa_rows // wb_rows
    return pl.pallas_call(
        _outproj_kernel,
        grid=(m // tm, d // tn),
        in_specs=[
            pl.BlockSpec((tm, tn), lambda i, j: (i, j)),
            pl.BlockSpec((tm, wa_rows), lambda i, j: (i, 0)),
            pl.BlockSpec((tm, wb_rows), lambda i, j: (i, 0)),
            pl.BlockSpec((tm, wb_rows), lambda i, j: (i, 0)),
            pl.BlockSpec((wa_rows, tn), lambda i, j: (0, j)),
            pl.BlockSpec((wb_rows, tn), lambda i, j: (nb, j)),
            pl.BlockSpec((wb_rows, tn), lambda i, j: (nb + 1, j)),
        ],
        out_specs=pl.BlockSpec((tm, tn), lambda i, j: (i, j)),
        out_shape=jax.ShapeDtypeStruct((m, d), F32),
        compiler_params=_params(("parallel", "arbitrary")),
        name="outproj",
    )(x2, a, b, c, w_o, w_o, w_o)


def _ffn_up_kernel(x_ref, xp_ref, xn_ref, g_ref, wa_ref, wg_ref, cwa_ref, cwg_ref, cba_ref, cbg_ref,
                   o_ref, h_scr, *, tiles_per_seq):
    i = pl.program_id(0)
    tm = x_ref.shape[0]

    @pl.when(pl.program_id(1) == 0)
    def _():
        g = g_ref[...]
        keep_prev = (i % tiles_per_seq != 0).astype(F32)
        keep_next = ((i + 1) % tiles_per_seq != 0).astype(F32)
        h_scr[0:HALO, :] = (_rms_rows(xp_ref[...], g) * keep_prev).astype(BF16)
        h_scr[HALO:HALO + tm, :] = _rms_rows(x_ref[...], g).astype(BF16)
        h_scr[HALO + tm:, :] = (_rms_rows(xn_ref[...], g) * keep_next).astype(BF16)

    h = h_scr[...]
    n_ext = tm + 2 * HALO

    def conv(w_ref, cw_ref, cb_ref):
        u = _dot(h, w_ref[...])
        prev = pltpu.roll(u, 1, 0)[HALO:HALO + tm]
        nxt = pltpu.roll(u, n_ext - 1, 0)[HALO:HALO + tm]
        return prev * cw_ref[0:1, :] + u[HALO:HALO + tm] * cw_ref[1:2, :] + nxt * cw_ref[2:3, :] + cb_ref[...]

    a = conv(wa_ref, cwa_ref, cba_ref)
    gt = conv(wg_ref, cwg_ref, cbg_ref)
    o_ref[...] = (a * _silu(gt)).astype(BF16)


def _ffn_up(x2, g, w_up, conv_w, conv_b, t):
    m, d = x2.shape
    d_ff = w_up.shape[1] // 2
    tm = min(MM_TILE_M // 2, t)
    tn = FFN_TILE_N
    nj = d_ff // tn
    blocks_per_tile = tm // HALO
    last_block = m // HALO - 1
    return pl.pallas_call(
        functools.partial(_ffn_up_kernel, tiles_per_seq=t // tm),
        grid=(m // tm, nj),
        in_specs=[
            pl.BlockSpec((tm, d), lambda i, j: (i, 0)),
            pl.BlockSpec((HALO, d), lambda i, j: (jnp.maximum(i * blocks_per_tile - 1, 0), 0)),
            pl.BlockSpec((HALO, d), lambda i, j: (jnp.minimum((i + 1) * blocks_per_tile, last_block), 0)),
            pl.BlockSpec((1, d), lambda i, j: (0, 0)),
            pl.BlockSpec((d, tn), lambda i, j: (0, j)),
            pl.BlockSpec((d, tn), lambda i, j: (0, nj + j)),
            pl.BlockSpec((CONV_W, tn), lambda i, j: (0, j)),
            pl.BlockSpec((CONV_W, tn), lambda i, j: (0, nj + j)),
            pl.BlockSpec((1, tn), lambda i, j: (0, j)),
            pl.BlockSpec((1, tn), lambda i, j: (0, nj + j)),
        ],
        out_specs=pl.BlockSpec((tm, tn), lambda i, j: (i, j)),
        out_shape=jax.ShapeDtypeStruct((m, d_ff), BF16),
        scratch_shapes=[pltpu.VMEM((tm + 2 * HALO, d), BF16)],
        compiler_params=_params(("parallel", "arbitrary")),
        name="ffn_up",
    )(x2, x2, x2, g, w_up, w_up, conv_w, conv_w, conv_b, conv_b)


def _ffn_down_kernel(x_ref, a_ref, w_ref, o_ref):
    o_ref[...] = x_ref[...] + _dot(a_ref[...], w_ref[...])


def _ffn_down(x2, act, w_down):
    m, d = x2.shape
    k = act.shape[1]
    tm, tn = min(MM_TILE_M, m), FFN_TILE_N
    return pl.pallas_call(
        _ffn_down_kernel,
        grid=(m // tm, d // tn),
        in_specs=[
            pl.BlockSpec((tm, tn), lambda i, j: (i, j)),
            pl.BlockSpec((tm, k), lambda i, j: (i, 0)),
            pl.BlockSpec((k, tn), lambda i, j: (0, j)),
        ],
        out_specs=pl.BlockSpec((tm, tn), lambda i, j: (i, j)),
        out_shape=jax.ShapeDtypeStruct((m, d), F32),
        compiler_params=_params(("parallel", "arbitrary")),
        name="ffn_down",
    )(x2, act, w_down)


def _rmsnorm_kernel(x_ref, g_ref, o_ref):
    o_ref[...] = _rms_rows(x_ref[...], g_ref[...])


def _rmsnorm(x2, g):
    m, d = x2.shape
    tm = min(MM_TILE_M, m)
    return pl.pallas_call(
        _rmsnorm_kernel,
        grid=(m // tm,),
        in_specs=[pl.BlockSpec((tm, d), lambda i: (i, 0)), pl.BlockSpec((1, d), lambda i: (0, 0))],
        out_specs=pl.BlockSpec((tm, d), lambda i: (i, 0)),
        out_shape=jax.ShapeDtypeStruct((m, d), F32),
        compiler_params=_params(("parallel",)),
        name="final_rmsnorm",
    )(x2, g)


def _rotary_tables(t):
    half = HEAD_K // 2
    inv = 1.0 / (ROPE_BASE ** jnp.linspace(0.0, 1.0, half, dtype=F32))
    ang = jnp.arange(t, dtype=jnp.int32).astype(F32)[:, None] * inv[None, :]
    cos, sin = jnp.cos(ang), jnp.sin(ang)
    cos_l = jnp.tile(cos, (1, 2 * PAIR_HEADS))
    sin_l = jnp.tile(jnp.concatenate([-sin, sin], axis=-1), (1, PAIR_HEADS))
    return cos_l, sin_l


def kernel(x, ln1_g, w_in, diff_lambda, rel_bias, gla_gate_w, gla_gate_b, ret_decay_logit, head_gain,
           w_o, ln2_g, w_up, conv_w, conv_b, w_down, final_g):
    batch, t, d = x.shape
    depth = w_in.shape[0]
    diff_w = DIFF_HEADS * HEAD_V
    gla_w = 2 * N_PAIRS * HEAD_V
    gla_k = 2 * N_PAIRS * HEAD_K
    ga0 = 3 * diff_w + 2 * gla_k + 2 * gla_w
    ga1 = ga0 + 2 * GLA_GATE_RANK
    gla_col0 = 3 * diff_w // LANES
    ret_col0 = ga0 // LANES

    w_main = jnp.concatenate([w_in[:, :, :ga0], w_in[:, :, ga1:]], axis=-1).astype(BF16)
    w_ga = jnp.pad(w_in[:, :, ga0:ga1], ((0, 0), (0, 0), (0, LANES - 2 * GLA_GATE_RANK))).astype(BF16)
    w_o16, w_up16, w_down16 = w_o.astype(BF16), w_up.astype(BF16), w_down.astype(BF16)
    k_scale = HEAD_K ** -0.5
    colscale = jnp.ones((1, w_main.shape[-1]), F32)
    colscale = colscale.at[:, :diff_w].set(k_scale * LOG2E)
    colscale = colscale.at[:, 3 * diff_w:3 * diff_w + gla_k].set(k_scale)
    colscale = colscale.at[:, ga0 + gla_k:ga0 + 2 * gla_k].set(k_scale)
    r = GLA_GATE_RANK
    gate_wf = jnp.pad(gla_gate_w[:, 0], ((0, 0), (0, LANES - r), (0, 0)))
    gate_wb = jnp.pad(gla_gate_w[:, 1], ((0, 0), (r, LANES - 2 * r), (0, 0)))
    ret_logit = jnp.repeat(ret_decay_logit, HEAD_K, axis=-1)

    rel = (jnp.arange(ATT_TILE, dtype=jnp.int32)[:, None] - jnp.arange(ATT_TILE, dtype=jnp.int32)[None, :])
    rel = rel[None] + ATT_TILE * (jnp.arange(5, dtype=jnp.int32) - 2)[:, None, None]
    bias_tiles = _bias_tiles(rel_bias.astype(F32), _t5_bucket(rel))
    cos_l, sin_l = _rotary_tables(t)

    x2 = x.reshape(batch * t, d)
    for layer in range(depth):
        lam_init = 0.8 - 0.6 * math.exp(-0.3 * layer)
        gain = head_gain[layer][None, :]
        proj, ga = _inproj(x2, ln1_g[layer][None, :], w_main[layer], w_ga[layer], colscale)
        a_out = _diff_attention(proj, bias_tiles, diff_lambda[layer], gain[:, :diff_w], lam_init, batch, t)
        b_out = _linattn("gla", proj, gla_col0, gain[:, diff_w:diff_w + gla_w],
                         (ga, gate_wf[layer], gate_wb[layer], gla_gate_b[layer]), batch, t)
        c_out = _linattn("ret", proj, ret_col0, gain[:, diff_w + gla_w:],
                         (cos_l, sin_l, ret_logit[layer]), batch, t)
        x2 = _outproj(x2, a_out, b_out, c_out, w_o16[layer])
        act = _ffn_up(x2, ln2_g[layer][None, :], w_up16[layer], conv_w[layer], conv_b[layer][None, :], t)
        x2 = _ffn_down(x2, act, w_down16[layer])
    return _rmsnorm(x2, final_g[None, :]).reshape(batch, t, d)
```

```python
import functools
import math

import jax
import jax.numpy as jnp
from jax import lax
from jax.experimental import pallas as pl
from jax.experimental.pallas import tpu as pltpu

F32 = jnp.float32
BF16 = jnp.bfloat16

HEAD_V = 128
DIFF_HEADS = 8
PAIR_HEADS = 2
N_PAIRS = 2
HEAD_K = 64
GLA_GATE_RANK = 16
GLA_TAU = 16.0
ROPE_BASE = 10000.0
N_BUCKETS = 32
MAX_DISTANCE = 128
CONV_W = 3
EPS = 1e-6
LOG2E = math.log2(math.e)

LANES = 128
SUBLANES = 8
BF16_ROWS = 16
VMEM_LIMIT_BYTES = 56 * 1024 * 1024

ATT_TILE = 256
ATT_QTILE = 1024
ATT_KSTAGE = 1024
GLA_CHUNK = 64
RET_CHUNK = 128
MM_TILE_M = 1024
MM_TILE_N = 1024
FFN_TILE_N = 512
HALO = SUBLANES

NEG_BIG = -1e30

_NT = (((1,), (1,)), ((), ()))
_TN = (((0,), (0,)), ((), ()))


def _params(sem):
    return pltpu.CompilerParams(dimension_semantics=sem, vmem_limit_bytes=VMEM_LIMIT_BYTES)


def _dot(a, b):
    return jnp.dot(a, b, preferred_element_type=F32)


def _dot_f32(a, b):
    return jnp.dot(a, b, preferred_element_type=F32, precision=lax.Precision.HIGHEST)


def _rms_rows(x, g):
    return x * lax.rsqrt(jnp.mean(x * x, axis=-1, keepdims=True) + EPS) * g


def _log_sigmoid(z):
    return jnp.minimum(z, 0.0) - jnp.log1p(jnp.exp(-jnp.abs(z)))


def _silu(z):
    return z * (1.0 / (1.0 + jnp.exp(-z)))


def _inproj_kernel(x_ref, g_ref, w_ref, wga_ref, cs_ref, proj_ref, ga_ref, h_scr):
    @pl.when(pl.program_id(1) == 0)
    def _():
        hb = _rms_rows(x_ref[...], g_ref[...]).astype(BF16)
        h_scr[...] = hb
        ga_ref[...] = _dot(hb, wga_ref[...])

    proj_ref[...] = (_dot(h_scr[...], w_ref[...]) * cs_ref[...]).astype(BF16)


def _inproj(x2, g, w, wga, colscale, layer):
    m, d = x2.shape
    n = w.shape[-1]
    tm, tn = min(MM_TILE_M, m), min(MM_TILE_N, n)
    return pl.pallas_call(
        _inproj_kernel,
        grid=(m // tm, n // tn),
        in_specs=[
            pl.BlockSpec((tm, d), lambda i, j: (i, 0)),
            pl.BlockSpec((1, d), lambda i, j: (0, 0)),
            pl.BlockSpec((None, d, tn), lambda i, j: (layer, 0, j)),
            pl.BlockSpec((None, d, LANES), lambda i, j: (layer, 0, 0)),
            pl.BlockSpec((1, tn), lambda i, j: (0, j)),
        ],
        out_specs=[
            pl.BlockSpec((tm, tn), lambda i, j: (i, j)),
            pl.BlockSpec((tm, LANES), lambda i, j: (i, 0)),
        ],
        out_shape=[jax.ShapeDtypeStruct((m, n), BF16), jax.ShapeDtypeStruct((m, LANES), F32)],
        scratch_shapes=[pltpu.VMEM((tm, d), BF16)],
        compiler_params=_params(("parallel", "arbitrary")),
        name="inproj",
    )(x2, g, w, wga, colscale)


def _bias_tiles_kernel(rb_ref, bucket_ref, out_ref):
    h = pl.program_id(0)
    bucket = bucket_ref[...]
    acc = jnp.zeros(bucket.shape, F32)
    for b in range(N_BUCKETS):
        acc = jnp.where(bucket == b, rb_ref[b, h], acc)
    out_ref[...] = acc * LOG2E


def _bias_tiles(rel_bias, bucket):
    nd = bucket.shape[0]
    return pl.pallas_call(
        _bias_tiles_kernel,
        grid=(DIFF_HEADS,),
        in_specs=[
            pl.BlockSpec(memory_space=pltpu.SMEM),
            pl.BlockSpec((nd, ATT_TILE, ATT_TILE), lambda h: (0, 0, 0)),
        ],
        out_specs=pl.BlockSpec((None, nd, ATT_TILE, ATT_TILE), lambda h: (h, 0, 0, 0)),
        out_shape=jax.ShapeDtypeStruct((DIFF_HEADS, nd, ATT_TILE, ATT_TILE), F32),
        compiler_params=_params(("arbitrary",)),
        name="bias_tiles",
    )(rel_bias, bucket)


def _t5_bucket(rel):
    half = N_BUCKETS // 2
    max_exact = half // 2
    ret = jnp.where(rel > 0, half, 0)
    n = jnp.abs(rel)
    nf = jnp.maximum(n, 1).astype(F32)
    large = max_exact + (jnp.log(nf / max_exact) / math.log(MAX_DISTANCE / max_exact)
                         * (half - max_exact)).astype(jnp.int32)
    large = jnp.minimum(large, half - 1)
    return ret + jnp.where(n < max_exact, n, large)


ACC_ROWS = HEAD_V + BF16_ROWS


def _attn_kernel(q_ref, k_ref, v_ref, bias_ref, lam_ref, gain_ref, o_ref,
                 vt_scr, qh_scr, s_scr, p_scr, mloc_scr, alpha_scr, m_scr, acc_scr,
                 *, lam_init, n_stages, stage_keys, q_tile_rows):
    tile = ATT_TILE
    tq, ks = q_tile_rows, stage_keys
    n_qb = tq // tile
    n_kb = ks // tile
    ones_rows = (lax.broadcasted_iota(jnp.int32, (BF16_ROWS, ks), 0) == 0).astype(BF16)
    for c in range(n_stages):
        for kb in range(n_kb):
            vc = v_ref[c * ks + kb * tile:c * ks + (kb + 1) * tile, :].astype(F32)
            vt_scr[c, 0:HEAD_V, kb * tile:(kb + 1) * tile] = vc.T.astype(BF16)
        vt_scr[c, HEAD_V:ACC_ROWS, :] = ones_rows

    lp = lam_ref[...]
    lam = (jnp.exp(jnp.sum(lp[0:1] * lp[1:2], axis=-1, keepdims=True))
           - jnp.exp(jnp.sum(lp[2:3] * lp[3:4], axis=-1, keepdims=True)) + lam_init)
    first_half = lax.broadcasted_iota(jnp.int32, (1, 2 * HEAD_K), 1) < HEAD_K
    gain = gain_ref[...] * (1.0 - lam_init)

    def q_tile(qi, _):
        rows = pl.ds(pl.multiple_of(qi * tq, tq), tq)
        q = q_ref[rows, :]
        zero = jnp.zeros_like(q)
        qh_scr[0] = jnp.where(first_half, q, zero)
        qh_scr[1] = jnp.where(first_half, zero, q)
        for half in range(2):
            m_scr[half] = jnp.full((1, tq), NEG_BIG, F32)
            acc_scr[half] = jnp.zeros((ACC_ROWS, tq), F32)

        def stage_a(c):
            kc = k_ref[pl.ds(pl.multiple_of(c * ks, ks), ks), :]
            for qb in range(n_qb):
                cols = slice(qb * tile, (qb + 1) * tile)
                for half in range(2):
                    s = lax.dot_general(kc, qh_scr[half, cols, :], _NT, preferred_element_type=F32)
                    mloc = None
                    for kb in range(n_kb):
                        krows = slice(kb * tile, (kb + 1) * tile)
                        bias = bias_ref[jnp.clip((c * n_kb + kb) - (qi * n_qb + qb), -2, 2) + 2]
                        sb = s[krows] + bias
                        s_scr[half, krows, cols] = sb
                        bmax = jnp.max(sb, axis=0, keepdims=True)
                        mloc = bmax if mloc is None else jnp.maximum(mloc, bmax)
                    mloc_scr[half, :, cols] = mloc

        def stage_b():
            for half in range(2):
                m_old = m_scr[half]
                m_new = jnp.maximum(m_old, mloc_scr[half])
                alpha_scr[half] = jnp.exp2(m_old - m_new)
                m_scr[half] = m_new
                p_scr[half] = jnp.exp2(s_scr[half] - m_new).astype(BF16)

        def stage_c(c):
            vt = vt_scr[c]
            for half in range(2):
                acc_scr[half] = acc_scr[half] * alpha_scr[half] + _dot(vt, p_scr[half])

        stage_a(0)
        stage_b()
        stage_a(1)

        def body(c, _):
            stage_c(c)
            stage_b()
            stage_a(c + 2)
            return 0

        lax.fori_loop(0, n_stages - 2, body, 0)
        stage_c(n_stages - 2)
        stage_b()
        stage_c(n_stages - 1)

        acc1, acc2 = acc_scr[0], acc_scr[1]
        o1 = acc1[0:HEAD_V] / acc1[HEAD_V:HEAD_V + 1]
        o2 = acc2[0:HEAD_V] / acc2[HEAD_V:HEAD_V + 1]
        ot = o1 - lam * o2
        ot = ot * lax.rsqrt(jnp.mean(ot * ot, axis=0, keepdims=True) + EPS)
        o_ref[rows, :] = (ot.T * gain).astype(BF16)
        return 0

    lax.fori_loop(0, (n_stages * ks) // tq, q_tile, 0)


def _diff_attention(proj, bias_tiles, lam_params, gain, lam_init, batch, t):
    nd = bias_tiles.shape[1]
    tile = ATT_TILE
    tq = min(ATT_QTILE, t)
    ks = min(ATT_KSTAGE, t // 2)
    n_stages = t // ks
    return pl.pallas_call(
        functools.partial(_attn_kernel, lam_init=lam_init, n_stages=n_stages, stage_keys=ks, q_tile_rows=tq),
        grid=(batch, DIFF_HEADS),
        in_specs=[
            pl.BlockSpec((t, HEAD_V), lambda b, h: (b, h)),
            pl.BlockSpec((t, HEAD_V), lambda b, h: (b, DIFF_HEADS + h)),
            pl.BlockSpec((t, HEAD_V), lambda b, h: (b, 2 * DIFF_HEADS + h)),
            pl.BlockSpec((None, nd, tile, tile), lambda b, h: (h, 0, 0, 0)),
            pl.BlockSpec(lam_params.shape, lambda b, h: (0, 0)),
            pl.BlockSpec((1, HEAD_V), lambda b, h: (0, h)),
        ],
        out_specs=pl.BlockSpec((t, HEAD_V), lambda b, h: (b, h)),
        out_shape=jax.ShapeDtypeStruct((batch * t, DIFF_HEADS * HEAD_V), BF16),
        scratch_shapes=[
            pltpu.VMEM((n_stages, ACC_ROWS, ks), BF16),
            pltpu.VMEM((2, tq, 2 * HEAD_K), BF16),
            pltpu.VMEM((2, ks, tq), F32),
            pltpu.VMEM((2, ks, tq), BF16),
            pltpu.VMEM((2, 1, tq), F32),
            pltpu.VMEM((2, 1, tq), F32),
            pltpu.VMEM((2, 1, tq), F32),
            pltpu.VMEM((2, ACC_ROWS, tq), F32),
        ],
        compiler_params=_params(("parallel", "arbitrary")),
        name="diff_attn",
    )(proj, proj, proj, bias_tiles, lam_params, gain)


def _linattn_kernel(*refs, mode, chunk, n_chunks):
    if mode == "gla":
        (q_ref, k_ref, v_ref, gate_ref, gain_ref, ga_ref, wf_ref, wb_ref, gb_ref,
         o_ref, b_scr, e_scr, tot_scr, kvf_scr, rnext_scr) = refs
    else:
        (q_ref, k_ref, v_ref, gate_ref, gain_ref, cos_ref, sin_ref, logit_ref,
         o_ref, kx_scr, kvf_scr, rnext_scr) = refs
    c = chunk
    kw = PAIR_HEADS * HEAD_K
    vw = PAIR_HEADS * HEAD_V

    lane_k = lax.broadcasted_iota(jnp.int32, (1, kw), 1)
    head0_k = lane_k < HEAD_K
    st_row = lax.broadcasted_iota(jnp.int32, (vw, kw), 0)
    st_lane = lax.broadcasted_iota(jnp.int32, (vw, kw), 1)
    same_head = (st_row < HEAD_V) == (st_lane < HEAD_K)
    ri = lax.broadcasted_iota(jnp.int32, (c, c), 0)
    ci = lax.broadcasted_iota(jnp.int32, (c, c), 1)
    lower, upper = ri >= ci, ri <= ci

    def chunk_rows(n):
        return pl.ds(pl.multiple_of(n * c, c), c)

    if mode == "ret":
        pr = lax.broadcasted_iota(jnp.int32, (kw, kw), 0)
        pc = lax.broadcasted_iota(jnp.int32, (kw, kw), 1)
        swap = (pr == (pc ^ (HEAD_K // 2))).astype(BF16)

        def rotary(x_bf16, rows):
            xr = _dot(x_bf16, swap)
            return x_bf16.astype(F32) * cos_ref[rows, :] + xr * sin_ref[rows, :]

        lg = _log_sigmoid(logit_ref[...])
        lgf, lgb = lg[0:1], lg[1:2]
        pos = lax.broadcasted_iota(jnp.int32, (c, kw), 0).astype(F32)
        b_const, e_const = (pos + 1.0) * lgf, pos * lgb
        totf_const, totb_const = float(c) * lgf, float(c) * lgb
        rel = (ri - ci).astype(F32)
        dmask = []
        for h in range(PAIR_HEADS):
            lf = lgf[:, h * HEAD_K:h * HEAD_K + 1]
            lb = lgb[:, h * HEAD_K:h * HEAD_K + 1]
            dmask.append(jnp.where(lower, jnp.exp(lf * jnp.maximum(rel, 0.0)), 0.0)
                         + jnp.where(upper, jnp.exp(lb * jnp.maximum(-rel, 0.0)), 0.0))
    else:
        incl = lower.astype(F32)
        excl = (ri > ci).astype(F32)

    def phase1(i, r_state):
        n = n_chunks - 1 - i
        rows = chunk_rows(n)
        v = v_ref[rows, :]
        if mode == "gla":
            k = k_ref[rows, :].astype(F32)
            ga = ga_ref[rows, :]
            laf = _log_sigmoid(_dot_f32(ga, wf_ref[...]) + gb_ref[0:1, :]) * (1.0 / GLA_TAU)
            lab = _log_sigmoid(_dot_f32(ga, wb_ref[...]) + gb_ref[1:2, :]) * (1.0 / GLA_TAU)
            b = _dot_f32(incl, laf)
            e = _dot_f32(excl, lab)
            totf = b[c - 1:c, :]
            totb = e[c - 1:c, :] + lab[c - 1:c, :]
            b_scr[rows, :] = b
            e_scr[rows, :] = e
            tot_scr[n, 0:SUBLANES, :] = jnp.broadcast_to(totf, (SUBLANES, kw))
            tot_scr[n, SUBLANES:2 * SUBLANES, :] = jnp.broadcast_to(totb, (SUBLANES, kw))
        else:
            k = rotary(k_ref[rows, :], rows)
            kx_scr[rows, :] = k
            b, e, totf, totb = b_const, e_const, totf_const, totb_const
        kf = (k * jnp.exp(totf - b)).astype(BF16)
        kb = (k * jnp.exp(e)).astype(BF16)
        kvf = lax.dot_general(v, kf, _TN, preferred_element_type=F32)
        kvb = lax.dot_general(v, kb, _TN, preferred_element_type=F32)
        kvf_scr[n] = jnp.where(same_head, kvf, 0.0)
        rnext_scr[n] = r_state.astype(BF16)
        return jnp.exp(totb) * r_state + jnp.where(same_head, kvb, 0.0)

    lax.fori_loop(0, n_chunks, phase1, jnp.zeros((vw, kw), F32))

    def phase2(n, s_state):
        rows = chunk_rows(n)
        v = v_ref[rows, :]
        if mode == "gla":
            q = q_ref[rows, :].astype(F32)
            k = k_ref[rows, :].astype(F32)
            b, e = b_scr[rows, :], e_scr[rows, :]
            totf, totb = tot_scr[n, 0:1, :], tot_scr[n, SUBLANES:SUBLANES + 1, :]
        else:
            q = rotary(q_ref[rows, :], rows)
            k = kx_scr[rows, :]
            b, e, totf, totb = b_const, e_const, totf_const, totb_const

        qf = (q * jnp.exp(b)).astype(BF16)
        qb = (q * jnp.exp(totb - e)).astype(BF16)
        inter = (lax.dot_general(qf, s_state.astype(BF16), _NT, preferred_element_type=F32)
                 + lax.dot_general(qb, rnext_scr[n], _NT, preferred_element_type=F32))

        if mode == "gla":
            bm = b[c // 2 - 1:c // 2, :]
            em = e[c // 2:c // 2 + 1, :]
            qtf, ktf = q * jnp.exp(b - bm), (k * jnp.exp(bm - b)).astype(BF16)
            qtb, ktb = q * jnp.exp(em - e), (k * jnp.exp(e - em)).astype(BF16)
        else:
            kb16 = k.astype(BF16)
        outs = []
        for h in range(PAIR_HEADS):
            mh = head0_k if h == 0 else jnp.logical_not(head0_k)
            if mode == "gla":
                sf = lax.dot_general(jnp.where(mh, qtf, 0.0).astype(BF16), ktf, _NT, preferred_element_type=F32)
                sb = lax.dot_general(jnp.where(mh, qtb, 0.0).astype(BF16), ktb, _NT, preferred_element_type=F32)
                sc = jnp.where(lower, sf, 0.0) + jnp.where(upper, sb, 0.0)
            else:
                sc = lax.dot_general(jnp.where(mh, q, 0.0).astype(BF16), kb16, _NT,
                                     preferred_element_type=F32) * dmask[h]
            outs.append(_dot(sc.astype(BF16), v[:, h * HEAD_V:(h + 1) * HEAD_V]))

        gate = gate_ref[rows, :].astype(F32)
        for h in range(PAIR_HEADS):
            cols = slice(h * HEAD_V, (h + 1) * HEAD_V)
            o = outs[h] + inter[:, cols]
            o = _rms_rows(o, gain_ref[:, cols]) * _silu(gate[:, cols])
            o_ref[rows, cols] = o.astype(BF16)

        kvf = kvf_scr[n]
        return jnp.exp(totf) * s_state + kvf

    lax.fori_loop(0, n_chunks, phase2, jnp.zeros((vw, kw), F32))


def _linattn(mode, proj, col0, gain, extra, batch, t):
    chunk = GLA_CHUNK if mode == "gla" else RET_CHUNK
    n_chunks = t // chunk
    kw, vw = PAIR_HEADS * HEAD_K, PAIR_HEADS * HEAD_V
    v0 = (col0 + 2 * N_PAIRS) // 2
    g0 = v0 + N_PAIRS
    in_specs = [
        pl.BlockSpec((t, kw), lambda b, p: (b, col0 + p)),
        pl.BlockSpec((t, kw), lambda b, p: (b, col0 + N_PAIRS + p)),
        pl.BlockSpec((t, vw), lambda b, p: (b, v0 + p)),
        pl.BlockSpec((t, vw), lambda b, p: (b, g0 + p)),
        pl.BlockSpec((1, vw), lambda b, p: (0, p)),
    ]
    args = [proj, proj, proj, proj, gain]
    state_scratch = [pltpu.VMEM((n_chunks, vw, kw), F32), pltpu.VMEM((n_chunks, vw, kw), BF16)]
    if mode == "gla":
        ga, wf, wb, gb = extra
        in_specs += [
            pl.BlockSpec((t, LANES), lambda b, p: (b, 0)),
            pl.BlockSpec((LANES, kw), lambda b, p: (0, p)),
            pl.BlockSpec((LANES, kw), lambda b, p: (0, p)),
            pl.BlockSpec((2, kw), lambda b, p: (0, p)),
        ]
        args += [ga, wf, wb, gb]
        scratch = [pltpu.VMEM((t, kw), F32), pltpu.VMEM((t, kw), F32),
                   pltpu.VMEM((n_chunks, 2 * SUBLANES, kw), F32)] + state_scratch
    else:
        cos, sin, logit = extra
        in_specs += [
            pl.BlockSpec((t, kw), lambda b, p: (0, 0)),
            pl.BlockSpec((t, kw), lambda b, p: (0, 0)),
            pl.BlockSpec((2, kw), lambda b, p: (0, p)),
        ]
        args += [cos, sin, logit]
        scratch = [pltpu.VMEM((t, kw), F32)] + state_scratch
    return pl.pallas_call(
        functools.partial(_linattn_kernel, mode=mode, chunk=chunk, n_chunks=n_chunks),
        grid=(batch, N_PAIRS),
        in_specs=in_specs,
        out_specs=pl.BlockSpec((t, vw), lambda b, p: (b, p)),
        out_shape=jax.ShapeDtypeStruct((batch * t, N_PAIRS * vw), BF16),
        scratch_shapes=scratch,
        compiler_params=_params(("parallel", "arbitrary")),
        name="linattn_" + mode,
    )(*args)


def _outproj_kernel(x_ref, a_ref, b_ref, c_ref, wa_ref, wb_ref, wc_ref, o_ref):
    o_ref[...] = (x_ref[...] + _dot(a_ref[...], wa_ref[...]) + _dot(b_ref[...], wb_ref[...])
                  + _dot(c_ref[...], wc_ref[...]))


def _outproj(x2, a, b, c, w_o, layer):
    m, d = x2.shape
    tm, tn = min(MM_TILE_M, m), min(MM_TILE_N, d)
    wa_rows, wb_rows = a.shape[1], b.shape[1]
    nb = wa_rows // wb_rows
    return pl.pallas_call(
        _outproj_kernel,
        grid=(m // tm, d // tn),
        in_specs=[
            pl.BlockSpec((tm, tn), lambda i, j: (i, j)),
            pl.BlockSpec((tm, wa_rows), lambda i, j: (i, 0)),
            pl.BlockSpec((tm, wb_rows), lambda i, j: (i, 0)),
            pl.BlockSpec((tm, wb_rows), lambda i, j: (i, 0)),
            pl.BlockSpec((None, wa_rows, tn), lambda i, j: (layer, 0, j)),
            pl.BlockSpec((None, wb_rows, tn), lambda i, j: (layer, nb, j)),
            pl.BlockSpec((None, wb_rows, tn), lambda i, j: (layer, nb + 1, j)),
        ],
        out_specs=pl.BlockSpec((tm, tn), lambda i, j: (i, j)),
        out_shape=jax.ShapeDtypeStruct((m, d), F32),
        compiler_params=_params(("parallel", "arbitrary")),
        name="outproj",
    )(x2, a, b, c, w_o, w_o, w_o)


def _ffn_up_kernel(x_ref, xp_ref, xn_ref, g_ref, wa_ref, wg_ref, cwa_ref, cwg_ref, cba_ref, cbg_ref,
                   o_ref, h_scr, *, tiles_per_seq):
    i = pl.program_id(0)
    tm = x_ref.shape[0]

    @pl.when(pl.program_id(1) == 0)
    def _():
        g = g_ref[...]
        keep_prev = (i % tiles_per_seq != 0).astype(F32)
        keep_next = ((i + 1) % tiles_per_seq != 0).astype(F32)
        h_scr[0:HALO, :] = (_rms_rows(xp_ref[...], g) * keep_prev).astype(BF16)
        h_scr[HALO:HALO + tm, :] = _rms_rows(x_ref[...], g).astype(BF16)
        h_scr[HALO + tm:, :] = (_rms_rows(xn_ref[...], g) * keep_next).astype(BF16)

    h = h_scr[...]
    n_ext = tm + 2 * HALO

    def conv(w_ref, cw_ref, cb_ref):
        u = _dot(h, w_ref[...])
        prev = pltpu.roll(u, 1, 0)[HALO:HALO + tm]
        nxt = pltpu.roll(u, n_ext - 1, 0)[HALO:HALO + tm]
        return prev * cw_ref[0:1, :] + u[HALO:HALO + tm] * cw_ref[1:2, :] + nxt * cw_ref[2:3, :] + cb_ref[...]

    a = conv(wa_ref, cwa_ref, cba_ref)
    gt = conv(wg_ref, cwg_ref, cbg_ref)
    o_ref[...] = (a * _silu(gt)).astype(BF16)


def _ffn_up(x2, g, w_up, conv_w, conv_b, t, layer):
    m, d = x2.shape
    d_ff = w_up.shape[-1] // 2
    tm = min(MM_TILE_M // 2, t)
    tn = FFN_TILE_N
    nj = d_ff // tn
    blocks_per_tile = tm // HALO
    last_block = m // HALO - 1
    return pl.pallas_call(
        functools.partial(_ffn_up_kernel, tiles_per_seq=t // tm),
        grid=(m // tm, nj),
        in_specs=[
            pl.BlockSpec((tm, d), lambda i, j: (i, 0)),
            pl.BlockSpec((HALO, d), lambda i, j: (jnp.maximum(i * blocks_per_tile - 1, 0), 0)),
            pl.BlockSpec((HALO, d), lambda i, j: (jnp.minimum((i + 1) * blocks_per_tile, last_block), 0)),
            pl.BlockSpec((1, d), lambda i, j: (0, 0)),
            pl.BlockSpec((None, d, tn), lambda i, j: (layer, 0, j)),
            pl.BlockSpec((None, d, tn), lambda i, j: (layer, 0, nj + j)),
            pl.BlockSpec((CONV_W, tn), lambda i, j: (0, j)),
            pl.BlockSpec((CONV_W, tn), lambda i, j: (0, nj + j)),
            pl.BlockSpec((1, tn), lambda i, j: (0, j)),
            pl.BlockSpec((1, tn), lambda i, j: (0, nj + j)),
        ],
        out_specs=pl.BlockSpec((tm, tn), lambda i, j: (i, j)),
        out_shape=jax.ShapeDtypeStruct((m, d_ff), BF16),
        scratch_shapes=[pltpu.VMEM((tm + 2 * HALO, d), BF16)],
        compiler_params=_params(("parallel", "arbitrary")),
        name="ffn_up",
    )(x2, x2, x2, g, w_up, w_up, conv_w, conv_w, conv_b, conv_b)


def _ffn_down_kernel(x_ref, a_ref, w_ref, o_ref):
    o_ref[...] = x_ref[...] + _dot(a_ref[...], w_ref[...])


def _ffn_down(x2, act, w_down, layer):
    m, d = x2.shape
    k = act.shape[1]
    tm, tn = min(MM_TILE_M, m), FFN_TILE_N
    return pl.pallas_call(
        _ffn_down_kernel,
        grid=(m // tm, d // tn),
        in_specs=[
            pl.BlockSpec((tm, tn), lambda i, j: (i, j)),
            pl.BlockSpec((tm, k), lambda i, j: (i, 0)),
            pl.BlockSpec((None, k, tn), lambda i, j: (layer, 0, j)),
        ],
        out_specs=pl.BlockSpec((tm, tn), lambda i, j: (i, j)),
        out_shape=jax.ShapeDtypeStruct((m, d), F32),
        compiler_params=_params(("parallel", "arbitrary")),
        name="ffn_down",
    )(x2, act, w_down)


def _rmsnorm_kernel(x_ref, g_ref, o_ref):
    o_ref[...] = _rms_rows(x_ref[...], g_ref[...])


def _rmsnorm(x2, g):
    m, d = x2.shape
    tm = min(MM_TILE_M, m)
    return pl.pallas_call(
        _rmsnorm_kernel,
        grid=(m // tm,),
        in_specs=[pl.BlockSpec((tm, d), lambda i: (i, 0)), pl.BlockSpec((1, d), lambda i: (0, 0))],
        out_specs=pl.BlockSpec((tm, d), lambda i: (i, 0)),
        out_shape=jax.ShapeDtypeStruct((m, d), F32),
        compiler_params=_params(("parallel",)),
        name="final_rmsnorm",
    )(x2, g)


def _rotary_tables(t):
    half = HEAD_K // 2
    inv = 1.0 / (ROPE_BASE ** jnp.linspace(0.0, 1.0, half, dtype=F32))
    ang = jnp.arange(t, dtype=jnp.int32).astype(F32)[:, None] * inv[None, :]
    cos, sin = jnp.cos(ang), jnp.sin(ang)
    cos_l = jnp.tile(cos, (1, 2 * PAIR_HEADS))
    sin_l = jnp.tile(jnp.concatenate([-sin, sin], axis=-1), (1, PAIR_HEADS))
    return cos_l, sin_l


def kernel(x, ln1_g, w_in, diff_lambda, rel_bias, gla_gate_w, gla_gate_b, ret_decay_logit, head_gain,
           w_o, ln2_g, w_up, conv_w, conv_b, w_down, final_g):
    batch, t, d = x.shape
    depth = w_in.shape[0]
    diff_w = DIFF_HEADS * HEAD_V
    gla_w = 2 * N_PAIRS * HEAD_V
    gla_k = 2 * N_PAIRS * HEAD_K
    ga0 = 3 * diff_w + 2 * gla_k + 2 * gla_w
    ga1 = ga0 + 2 * GLA_GATE_RANK
    gla_col0 = 3 * diff_w // LANES
    ret_col0 = ga0 // LANES

    w_main = jnp.concatenate([w_in[:, :, :ga0], w_in[:, :, ga1:]], axis=-1).astype(BF16)
    w_ga = jnp.pad(w_in[:, :, ga0:ga1], ((0, 0), (0, 0), (0, LANES - 2 * GLA_GATE_RANK))).astype(BF16)
    w_o16, w_up16, w_down16 = w_o.astype(BF16), w_up.astype(BF16), w_down.astype(BF16)
    k_scale = HEAD_K ** -0.5
    colscale = jnp.ones((1, w_main.shape[-1]), F32)
    colscale = colscale.at[:, :diff_w].set(k_scale * LOG2E)
    colscale = colscale.at[:, 3 * diff_w:3 * diff_w + gla_k].set(k_scale)
    colscale = colscale.at[:, ga0 + gla_k:ga0 + 2 * gla_k].set(k_scale)
    r = GLA_GATE_RANK
    gate_wf = jnp.pad(gla_gate_w[:, 0], ((0, 0), (0, LANES - r), (0, 0)))
    gate_wb = jnp.pad(gla_gate_w[:, 1], ((0, 0), (r, LANES - 2 * r), (0, 0)))
    ret_logit = jnp.repeat(ret_decay_logit, HEAD_K, axis=-1)

    rel = (jnp.arange(ATT_TILE, dtype=jnp.int32)[:, None] - jnp.arange(ATT_TILE, dtype=jnp.int32)[None, :])
    rel = rel[None] + ATT_TILE * (jnp.arange(5, dtype=jnp.int32) - 2)[:, None, None]
    bias_tiles = _bias_tiles(rel_bias.astype(F32), _t5_bucket(rel))
    cos_l, sin_l = _rotary_tables(t)

    x2 = x.reshape(batch * t, d)
    for layer in range(depth):
        lam_init = 0.8 - 0.6 * math.exp(-0.3 * layer)
        gain = head_gain[layer][None, :]
        proj, ga = _inproj(x2, ln1_g[layer][None, :], w_main, w_ga, colscale, layer)
        a_out = _diff_attention(proj, bias_tiles, diff_lambda[layer], gain[:, :diff_w], lam_init, batch, t)
        b_out = _linattn("gla", proj, gla_col0, gain[:, diff_w:diff_w + gla_w],
                         (ga, gate_wf[layer], gate_wb[layer], gla_gate_b[layer]), batch, t)
        c_out = _linattn("ret", proj, ret_col0, gain[:, diff_w + gla_w:],
                         (cos_l, sin_l, ret_logit[layer]), batch, t)
        x2 = _outproj(x2, a_out, b_out, c_out, w_o16, layer)
        act = _ffn_up(x2, ln2_g[layer][None, :], w_up16, conv_w[layer], conv_b[layer][None, :], t, layer)
        x2 = _ffn_down(x2, act, w_down16, layer)
    return _rmsnorm(x2, final_g[None, :]).reshape(batch, t, d)
```

```python
import functools
import math

import jax
import jax.numpy as jnp
from jax import lax
from jax.experimental import pallas as pl
from jax.experimental.pallas import tpu as pltpu

F32 = jnp.float32
BF16 = jnp.bfloat16

HEAD_V = 128
DIFF_HEADS = 8
PAIR_HEADS = 2
N_PAIRS = 2
HEAD_K = 64
GLA_GATE_RANK = 16
GLA_TAU = 16.0
ROPE_BASE = 10000.0
N_BUCKETS = 32
MAX_DISTANCE = 128
CONV_W = 3
EPS = 1e-6
LOG2E = math.log2(math.e)

LANES = 128
SUBLANES = 8
BF16_ROWS = 16
VMEM_LIMIT_BYTES = 56 * 1024 * 1024

ATT_TILE = 256
ATT_QTILE = 1024
ATT_KSTAGE = 1024
GLA_CHUNK = 64
RET_CHUNK = 128
LIN_UNROLL = 4
MM_TILE_M = 1024
MM_TILE_N = 1024
FFN_TILE_N = 512
HALO = SUBLANES

NEG_BIG = -1e30

_NT = (((1,), (1,)), ((), ()))
_TN = (((0,), (0,)), ((), ()))


def _params(sem):
    return pltpu.CompilerParams(dimension_semantics=sem, vmem_limit_bytes=VMEM_LIMIT_BYTES)


def _dot(a, b):
    return jnp.dot(a, b, preferred_element_type=F32)


def _dot_f32(a, b):
    return jnp.dot(a, b, preferred_element_type=F32, precision=lax.Precision.HIGHEST)


def _rms_rows(x, g):
    return x * lax.rsqrt(jnp.mean(x * x, axis=-1, keepdims=True) + EPS) * g


def _log_sigmoid(z):
    return jnp.minimum(z, 0.0) - jnp.log1p(jnp.exp(-jnp.abs(z)))


def _silu(z):
    return z * (1.0 / (1.0 + jnp.exp(-z)))


def _inproj_kernel(x_ref, g_ref, w_ref, wga_ref, cs_ref, proj_ref, ga_ref, h_scr):
    @pl.when(pl.program_id(1) == 0)
    def _():
        hb = _rms_rows(x_ref[...], g_ref[...]).astype(BF16)
        h_scr[...] = hb
        ga_ref[...] = _dot(hb, wga_ref[...])

    proj_ref[...] = (_dot(h_scr[...], w_ref[...]) * cs_ref[...]).astype(BF16)


def _inproj(x2, g, w, wga, colscale, layer):
    m, d = x2.shape
    n = w.shape[-1]
    tm, tn = min(MM_TILE_M, m), min(MM_TILE_N, n)
    return pl.pallas_call(
        _inproj_kernel,
        grid=(m // tm, n // tn),
        in_specs=[
            pl.BlockSpec((tm, d), lambda i, j: (i, 0)),
            pl.BlockSpec((1, d), lambda i, j: (0, 0)),
            pl.BlockSpec((None, d, tn), lambda i, j: (layer, 0, j)),
            pl.BlockSpec((None, d, LANES), lambda i, j: (layer, 0, 0)),
            pl.BlockSpec((1, tn), lambda i, j: (0, j)),
        ],
        out_specs=[
            pl.BlockSpec((tm, tn), lambda i, j: (i, j)),
            pl.BlockSpec((tm, LANES), lambda i, j: (i, 0)),
        ],
        out_shape=[jax.ShapeDtypeStruct((m, n), BF16), jax.ShapeDtypeStruct((m, LANES), F32)],
        scratch_shapes=[pltpu.VMEM((tm, d), BF16)],
        compiler_params=_params(("parallel", "arbitrary")),
        name="inproj",
    )(x2, g, w, wga, colscale)


def _bias_tiles_kernel(rb_ref, bucket_ref, out_ref):
    h = pl.program_id(0)
    bucket = bucket_ref[...]
    acc = jnp.zeros(bucket.shape, F32)
    for b in range(N_BUCKETS):
        acc = jnp.where(bucket == b, rb_ref[b, h], acc)
    out_ref[...] = acc * LOG2E


def _bias_tiles(rel_bias, bucket):
    nd = bucket.shape[0]
    return pl.pallas_call(
        _bias_tiles_kernel,
        grid=(DIFF_HEADS,),
        in_specs=[
            pl.BlockSpec(memory_space=pltpu.SMEM),
            pl.BlockSpec((nd, ATT_TILE, ATT_TILE), lambda h: (0, 0, 0)),
        ],
        out_specs=pl.BlockSpec((None, nd, ATT_TILE, ATT_TILE), lambda h: (h, 0, 0, 0)),
        out_shape=jax.ShapeDtypeStruct((DIFF_HEADS, nd, ATT_TILE, ATT_TILE), F32),
        compiler_params=_params(("arbitrary",)),
        name="bias_tiles",
    )(rel_bias, bucket)


def _t5_bucket(rel):
    half = N_BUCKETS // 2
    max_exact = half // 2
    ret = jnp.where(rel > 0, half, 0)
    n = jnp.abs(rel)
    nf = jnp.maximum(n, 1).astype(F32)
    large = max_exact + (jnp.log(nf / max_exact) / math.log(MAX_DISTANCE / max_exact)
                         * (half - max_exact)).astype(jnp.int32)
    large = jnp.minimum(large, half - 1)
    return ret + jnp.where(n < max_exact, n, large)


ACC_ROWS = HEAD_V + BF16_ROWS


def _attn_kernel(q_ref, k_ref, v_ref, bias_ref, lam_ref, gain_ref, o_ref,
                 vt_scr, qh_scr, s_scr, p_scr, mloc_scr, alpha_scr, m_scr, acc_scr,
                 *, lam_init, n_stages, stage_keys, q_tile_rows):
    tile = ATT_TILE
    tq, ks = q_tile_rows, stage_keys
    n_qb = tq // tile
    n_kb = ks // tile
    ones_rows = (lax.broadcasted_iota(jnp.int32, (BF16_ROWS, ks), 0) == 0).astype(BF16)
    for c in range(n_stages):
        for kb in range(n_kb):
            vc = v_ref[c * ks + kb * tile:c * ks + (kb + 1) * tile, :].astype(F32)
            vt_scr[c, 0:HEAD_V, kb * tile:(kb + 1) * tile] = vc.T.astype(BF16)
        vt_scr[c, HEAD_V:ACC_ROWS, :] = ones_rows

    lp = lam_ref[...]
    lam = (jnp.exp(jnp.sum(lp[0:1] * lp[1:2], axis=-1, keepdims=True))
           - jnp.exp(jnp.sum(lp[2:3] * lp[3:4], axis=-1, keepdims=True)) + lam_init)
    first_half = lax.broadcasted_iota(jnp.int32, (1, 2 * HEAD_K), 1) < HEAD_K
    gain = gain_ref[...] * (1.0 - lam_init)

    def q_tile(qi, _):
        rows = pl.ds(pl.multiple_of(qi * tq, tq), tq)
        q = q_ref[rows, :]
        zero = jnp.zeros_like(q)
        qh_scr[0] = jnp.where(first_half, q, zero)
        qh_scr[1] = jnp.where(first_half, zero, q)
        for half in range(2):
            m_scr[half] = jnp.full((1, tq), NEG_BIG, F32)
            acc_scr[half] = jnp.zeros((ACC_ROWS, tq), F32)

        def stage_a(c):
            kc = k_ref[pl.ds(pl.multiple_of(c * ks, ks), ks), :]
            for qb in range(n_qb):
                cols = slice(qb * tile, (qb + 1) * tile)
                for half in range(2):
                    s = lax.dot_general(kc, qh_scr[half, cols, :], _NT, preferred_element_type=F32)
                    mloc = None
                    for kb in range(n_kb):
                        krows = slice(kb * tile, (kb + 1) * tile)
                        bias = bias_ref[jnp.clip((c * n_kb + kb) - (qi * n_qb + qb), -2, 2) + 2]
                        sb = s[krows] + bias
                        s_scr[half, krows, cols] = sb
                        bmax = jnp.max(sb, axis=0, keepdims=True)
                        mloc = bmax if mloc is None else jnp.maximum(mloc, bmax)
                    mloc_scr[half, :, cols] = mloc

        def stage_b():
            for half in range(2):
                m_old = m_scr[half]
                m_new = jnp.maximum(m_old, mloc_scr[half])
                alpha_scr[half] = jnp.exp2(m_old - m_new)
                m_scr[half] = m_new
                p_scr[half] = jnp.exp2(s_scr[half] - m_new).astype(BF16)

        def stage_c(c):
            vt = vt_scr[c]
            for half in range(2):
                acc_scr[half] = acc_scr[half] * alpha_scr[half] + _dot(vt, p_scr[half])

        stage_a(0)
        stage_b()
        stage_a(1)

        def body(c, _):
            stage_c(c)
            stage_b()
            stage_a(c + 2)
            return 0

        lax.fori_loop(0, n_stages - 2, body, 0)
        stage_c(n_stages - 2)
        stage_b()
        stage_c(n_stages - 1)

        acc1, acc2 = acc_scr[0], acc_scr[1]
        o1 = acc1[0:HEAD_V] / acc1[HEAD_V:HEAD_V + 1]
        o2 = acc2[0:HEAD_V] / acc2[HEAD_V:HEAD_V + 1]
        ot = o1 - lam * o2
        ot = ot * lax.rsqrt(jnp.mean(ot * ot, axis=0, keepdims=True) + EPS)
        o_ref[rows, :] = (ot.T * gain).astype(BF16)
        return 0

    lax.fori_loop(0, (n_stages * ks) // tq, q_tile, 0)


def _diff_attention(proj, bias_tiles, lam_params, gain, lam_init, batch, t):
    nd = bias_tiles.shape[1]
    tile = ATT_TILE
    tq = min(ATT_QTILE, t)
    ks = min(ATT_KSTAGE, t // 2)
    n_stages = t // ks
    return pl.pallas_call(
        functools.partial(_attn_kernel, lam_init=lam_init, n_stages=n_stages, stage_keys=ks, q_tile_rows=tq),
        grid=(batch, DIFF_HEADS),
        in_specs=[
            pl.BlockSpec((t, HEAD_V), lambda b, h: (b, h)),
            pl.BlockSpec((t, HEAD_V), lambda b, h: (b, DIFF_HEADS + h)),
            pl.BlockSpec((t, HEAD_V), lambda b, h: (b, 2 * DIFF_HEADS + h)),
            pl.BlockSpec((None, nd, tile, tile), lambda b, h: (h, 0, 0, 0)),
            pl.BlockSpec(lam_params.shape, lambda b, h: (0, 0)),
            pl.BlockSpec((1, HEAD_V), lambda b, h: (0, h)),
        ],
        out_specs=pl.BlockSpec((t, HEAD_V), lambda b, h: (b, h)),
        out_shape=jax.ShapeDtypeStruct((batch * t, DIFF_HEADS * HEAD_V), BF16),
        scratch_shapes=[
            pltpu.VMEM((n_stages, ACC_ROWS, ks), BF16),
            pltpu.VMEM((2, tq, 2 * HEAD_K), BF16),
            pltpu.VMEM((2, ks, tq), F32),
            pltpu.VMEM((2, ks, tq), BF16),
            pltpu.VMEM((2, 1, tq), F32),
            pltpu.VMEM((2, 1, tq), F32),
            pltpu.VMEM((2, 1, tq), F32),
            pltpu.VMEM((2, ACC_ROWS, tq), F32),
        ],
        compiler_params=_params(("parallel", "arbitrary")),
        name="diff_attn",
    )(proj, proj, proj, bias_tiles, lam_params, gain)


def _linattn_kernel(*refs, mode, chunk, n_chunks):
    if mode == "gla":
        (q_ref, k_ref, v_ref, gate_ref, gain_ref, ga_ref, wf_ref, wb_ref, gb_ref,
         o_ref, b_scr, einc_scr, lab_scr, kvf_scr, rnext_scr) = refs
    else:
        (q_ref, k_ref, v_ref, gate_ref, gain_ref, cos_ref, sin_ref, logit_ref,
         o_ref, kx_scr, kvf_scr, rnext_scr) = refs
    c = chunk
    kw = PAIR_HEADS * HEAD_K
    vw = PAIR_HEADS * HEAD_V

    lane_k = lax.broadcasted_iota(jnp.int32, (1, kw), 1)
    head0_k = lane_k < HEAD_K
    st_row = lax.broadcasted_iota(jnp.int32, (vw, kw), 0)
    st_lane = lax.broadcasted_iota(jnp.int32, (vw, kw), 1)
    same_head = (st_row < HEAD_V) == (st_lane < HEAD_K)
    ri = lax.broadcasted_iota(jnp.int32, (c, c), 0)
    ci = lax.broadcasted_iota(jnp.int32, (c, c), 1)
    lower, upper = ri >= ci, ri <= ci

    def chunk_rows(n):
        return pl.ds(pl.multiple_of(n * c, c), c)

    if mode == "ret":
        pr = lax.broadcasted_iota(jnp.int32, (kw, kw), 0)
        pc = lax.broadcasted_iota(jnp.int32, (kw, kw), 1)
        swap = (pr == (pc ^ (HEAD_K // 2))).astype(BF16)

        def rotary(x_bf16, rows):
            xr = _dot(x_bf16, swap)
            return x_bf16.astype(F32) * cos_ref[rows, :] + xr * sin_ref[rows, :]

        lg = _log_sigmoid(logit_ref[...])
        lgf, lgb = lg[0:1], lg[1:2]
        pos = lax.broadcasted_iota(jnp.int32, (c, kw), 0).astype(F32)
        b_const, e_const = (pos + 1.0) * lgf, pos * lgb
        totf_const, totb_const = float(c) * lgf, float(c) * lgb
        rel = (ri - ci).astype(F32)
        dmask = []
        for h in range(PAIR_HEADS):
            lf = lgf[:, h * HEAD_K:h * HEAD_K + 1]
            lb = lgb[:, h * HEAD_K:h * HEAD_K + 1]
            dmask.append(jnp.where(lower, jnp.exp(lf * jnp.maximum(rel, 0.0)), 0.0)
                         + jnp.where(upper, jnp.exp(lb * jnp.maximum(-rel, 0.0)), 0.0))
    else:
        t_rows = n_chunks * c
        in_chunk = lax.broadcasted_iota(jnp.int32, (t_rows, kw), 0) & (c - 1)

        def chunk_cumsum(x):
            shift = 1
            while shift < c:
                x = x + jnp.where(in_chunk >= shift, pltpu.roll(x, shift, 0), 0.0)
                shift *= 2
            return x

        ga = ga_ref[...]
        laf = _log_sigmoid(_dot_f32(ga, wf_ref[...]) + gb_ref[0:1, :]) * (1.0 / GLA_TAU)
        lab = _log_sigmoid(_dot_f32(ga, wb_ref[...]) + gb_ref[1:2, :]) * (1.0 / GLA_TAU)
        b_scr[...] = chunk_cumsum(laf)
        einc_scr[...] = chunk_cumsum(lab)
        lab_scr[...] = lab

    def gla_decays(rows):
        b, einc = b_scr[rows, :], einc_scr[rows, :]
        return b, einc - lab_scr[rows, :], b[c - 1:c, :], einc[c - 1:c, :]

    def phase1(i, r_state):
        n = n_chunks - 1 - i
        rows = chunk_rows(n)
        v = v_ref[rows, :]
        if mode == "gla":
            k = k_ref[rows, :].astype(F32)
            b, e, totf, totb = gla_decays(rows)
        else:
            k = rotary(k_ref[rows, :], rows)
            kx_scr[rows, :] = k
            b, e, totf, totb = b_const, e_const, totf_const, totb_const
        kf = (k * jnp.exp(totf - b)).astype(BF16)
        kb = (k * jnp.exp(e)).astype(BF16)
        kvf = lax.dot_general(v, kf, _TN, preferred_element_type=F32)
        kvb = lax.dot_general(v, kb, _TN, preferred_element_type=F32)
        kvf_scr[n] = jnp.where(same_head, kvf, 0.0)
        rnext_scr[n] = r_state.astype(BF16)
        return jnp.exp(totb) * r_state + jnp.where(same_head, kvb, 0.0)

    lax.fori_loop(0, n_chunks, phase1, jnp.zeros((vw, kw), F32), unroll=LIN_UNROLL)

    def phase2(n, s_state):
        rows = chunk_rows(n)
        v = v_ref[rows, :]
        if mode == "gla":
            q = q_ref[rows, :].astype(F32)
            k = k_ref[rows, :].astype(F32)
            b, e, totf, totb = gla_decays(rows)
        else:
            q = rotary(q_ref[rows, :], rows)
            k = kx_scr[rows, :]
            b, e, totf, totb = b_const, e_const, totf_const, totb_const

        qf = (q * jnp.exp(b)).astype(BF16)
        qb = (q * jnp.exp(totb - e)).astype(BF16)
        inter = (lax.dot_general(qf, s_state.astype(BF16), _NT, preferred_element_type=F32)
                 + lax.dot_general(qb, rnext_scr[n], _NT, preferred_element_type=F32))

        if mode == "gla":
            bm = b[c // 2 - 1:c // 2, :]
            em = e[c // 2:c // 2 + 1, :]
            qtf, ktf = q * jnp.exp(b - bm), (k * jnp.exp(bm - b)).astype(BF16)
            qtb, ktb = q * jnp.exp(em - e), (k * jnp.exp(e - em)).astype(BF16)
        else:
            kb16 = k.astype(BF16)
        outs = []
        for h in range(PAIR_HEADS):
            mh = head0_k if h == 0 else jnp.logical_not(head0_k)
            if mode == "gla":
                sf = lax.dot_general(jnp.where(mh, qtf, 0.0).astype(BF16), ktf, _NT, preferred_element_type=F32)
                sb = lax.dot_general(jnp.where(mh, qtb, 0.0).astype(BF16), ktb, _NT, preferred_element_type=F32)
                sc = jnp.where(lower, sf, 0.0) + jnp.where(upper, sb, 0.0)
            else:
                sc = lax.dot_general(jnp.where(mh, q, 0.0).astype(BF16), kb16, _NT,
                                     preferred_element_type=F32) * dmask[h]
            outs.append(_dot(sc.astype(BF16), v[:, h * HEAD_V:(h + 1) * HEAD_V]))

        gate = gate_ref[rows, :].astype(F32)
        for h in range(PAIR_HEADS):
            cols = slice(h * HEAD_V, (h + 1) * HEAD_V)
            o = outs[h] + inter[:, cols]
            o = _rms_rows(o, gain_ref[:, cols]) * _silu(gate[:, cols])
            o_ref[rows, cols] = o.astype(BF16)

        kvf = kvf_scr[n]
        return jnp.exp(totf) * s_state + kvf

    lax.fori_loop(0, n_chunks, phase2, jnp.zeros((vw, kw), F32), unroll=LIN_UNROLL)


def _linattn(mode, proj, col0, gain, extra, batch, t):
    chunk = GLA_CHUNK if mode == "gla" else RET_CHUNK
    n_chunks = t // chunk
    kw, vw = PAIR_HEADS * HEAD_K, PAIR_HEADS * HEAD_V
    v0 = (col0 + 2 * N_PAIRS) // 2
    g0 = v0 + N_PAIRS
    in_specs = [
        pl.BlockSpec((t, kw), lambda b, p: (b, col0 + p)),
        pl.BlockSpec((t, kw), lambda b, p: (b, col0 + N_PAIRS + p)),
        pl.BlockSpec((t, vw), lambda b, p: (b, v0 + p)),
        pl.BlockSpec((t, vw), lambda b, p: (b, g0 + p)),
        pl.BlockSpec((1, vw), lambda b, p: (0, p)),
    ]
    args = [proj, proj, proj, proj, gain]
    state_scratch = [pltpu.VMEM((n_chunks, vw, kw), F32), pltpu.VMEM((n_chunks, vw, kw), BF16)]
    if mode == "gla":
        ga, wf, wb, gb = extra
        in_specs += [
            pl.BlockSpec((t, LANES), lambda b, p: (b, 0)),
            pl.BlockSpec((LANES, kw), lambda b, p: (0, p)),
            pl.BlockSpec((LANES, kw), lambda b, p: (0, p)),
            pl.BlockSpec((2, kw), lambda b, p: (0, p)),
        ]
        args += [ga, wf, wb, gb]
        scratch = [pltpu.VMEM((t, kw), F32)] * 3 + state_scratch
    else:
        cos, sin, logit = extra
        in_specs += [
            pl.BlockSpec((t, kw), lambda b, p: (0, 0)),
            pl.BlockSpec((t, kw), lambda b, p: (0, 0)),
            pl.BlockSpec((2, kw), lambda b, p: (0, p)),
        ]
        args += [cos, sin, logit]
        scratch = [pltpu.VMEM((t, kw), F32)] + state_scratch
    return pl.pallas_call(
        functools.partial(_linattn_kernel, mode=mode, chunk=chunk, n_chunks=n_chunks),
        grid=(batch, N_PAIRS),
        in_specs=in_specs,
        out_specs=pl.BlockSpec((t, vw), lambda b, p: (b, p)),
        out_shape=jax.ShapeDtypeStruct((batch * t, N_PAIRS * vw), BF16),
        scratch_shapes=scratch,
        compiler_params=_params(("parallel", "arbitrary")),
        name="linattn_" + mode,
    )(*args)


def _outproj_kernel(x_ref, a_ref, b_ref, c_ref, wa_ref, wb_ref, wc_ref, o_ref):
    o_ref[...] = (x_ref[...] + _dot(a_ref[...], wa_ref[...]) + _dot(b_ref[...], wb_ref[...])
                  + _dot(c_ref[...], wc_ref[...]))


def _outproj(x2, a, b, c, w_o, layer):
    m, d = x2.shape
    tm, tn = min(MM_TILE_M, m), min(MM_TILE_N, d)
    wa_rows, wb_rows = a.shape[1], b.shape[1]
    nb = wa_rows // wb_rows
    return pl.pallas_call(
        _outproj_kernel,
        grid=(m // tm, d // tn),
        in_specs=[
            pl.BlockSpec((tm, tn), lambda i, j: (i, j)),
            pl.BlockSpec((tm, wa_rows), lambda i, j: (i, 0)),
            pl.BlockSpec((tm, wb_rows), lambda i, j: (i, 0)),
            pl.BlockSpec((tm, wb_rows), lambda i, j: (i, 0)),
            pl.BlockSpec((None, wa_rows, tn), lambda i, j: (layer, 0, j)),
            pl.BlockSpec((None, wb_rows, tn), lambda i, j: (layer, nb, j)),
            pl.BlockSpec((None, wb_rows, tn), lambda i, j: (layer, nb + 1, j)),
        ],
        out_specs=pl.BlockSpec((tm, tn), lambda i, j: (i, j)),
        out_shape=jax.ShapeDtypeStruct((m, d), F32),
        compiler_params=_params(("parallel", "arbitrary")),
        name="outproj",
    )(x2, a, b, c, w_o, w_o, w_o)


def _ffn_up_kernel(x_ref, xp_ref, xn_ref, g_ref, wa_ref, wg_ref, cwa_ref, cwg_ref, cba_ref, cbg_ref,
                   o_ref, h_scr, *, tiles_per_seq):
    i = pl.program_id(0)
    tm = x_ref.shape[0]

    @pl.when(pl.program_id(1) == 0)
    def _():
        g = g_ref[...]
        keep_prev = (i % tiles_per_seq != 0).astype(F32)
        keep_next = ((i + 1) % tiles_per_seq != 0).astype(F32)
        h_scr[0:HALO, :] = (_rms_rows(xp_ref[...], g) * keep_prev).astype(BF16)
        h_scr[HALO:HALO + tm, :] = _rms_rows(x_ref[...], g).astype(BF16)
        h_scr[HALO + tm:, :] = (_rms_rows(xn_ref[...], g) * keep_next).astype(BF16)

    h = h_scr[...]
    n_ext = tm + 2 * HALO

    def conv(w_ref, cw_ref, cb_ref):
        u = _dot(h, w_ref[...])
        prev = pltpu.roll(u, 1, 0)[HALO:HALO + tm]
        nxt = pltpu.roll(u, n_ext - 1, 0)[HALO:HALO + tm]
        return prev * cw_ref[0:1, :] + u[HALO:HALO + tm] * cw_ref[1:2, :] + nxt * cw_ref[2:3, :] + cb_ref[...]

    gate = _silu(conv(wg_ref, cwg_ref, cbg_ref))
    o_ref[...] = (conv(wa_ref, cwa_ref, cba_ref) * gate).astype(BF16)


def _ffn_up(x2, g, w_up, conv_w, conv_b, t, layer):
    m, d = x2.shape
    d_ff = w_up.shape[-1] // 2
    tm = min(MM_TILE_M, t)
    tn = FFN_TILE_N
    nj = d_ff // tn
    blocks_per_tile = tm // HALO
    last_block = m // HALO - 1
    return pl.pallas_call(
        functools.partial(_ffn_up_kernel, tiles_per_seq=t // tm),
        grid=(m // tm, nj),
        in_specs=[
            pl.BlockSpec((tm, d), lambda i, j: (i, 0)),
            pl.BlockSpec((HALO, d), lambda i, j: (jnp.maximum(i * blocks_per_tile - 1, 0), 0)),
            pl.BlockSpec((HALO, d), lambda i, j: (jnp.minimum((i + 1) * blocks_per_tile, last_block), 0)),
            pl.BlockSpec((1, d), lambda i, j: (0, 0)),
            pl.BlockSpec((None, d, tn), lambda i, j: (layer, 0, j)),
            pl.BlockSpec((None, d, tn), lambda i, j: (layer, 0, nj + j)),
            pl.BlockSpec((CONV_W, tn), lambda i, j: (0, j)),
            pl.BlockSpec((CONV_W, tn), lambda i, j: (0, nj + j)),
            pl.BlockSpec((1, tn), lambda i, j: (0, j)),
            pl.BlockSpec((1, tn), lambda i, j: (0, nj + j)),
        ],
        out_specs=pl.BlockSpec((tm, tn), lambda i, j: (i, j)),
        out_shape=jax.ShapeDtypeStruct((m, d_ff), BF16),
        scratch_shapes=[pltpu.VMEM((tm + 2 * HALO, d), BF16)],
        compiler_params=_params(("parallel", "arbitrary")),
        name="ffn_up",
    )(x2, x2, x2, g, w_up, w_up, conv_w, conv_w, conv_b, conv_b)


def _ffn_down_kernel(x_ref, a_ref, w_ref, o_ref):
    o_ref[...] = x_ref[...] + _dot(a_ref[...], w_ref[...])


def _ffn_down(x2, act, w_down, layer):
    m, d = x2.shape
    k = act.shape[1]
    tm, tn = min(MM_TILE_M, m), FFN_TILE_N
    return pl.pallas_call(
        _ffn_down_kernel,
        grid=(m // tm, d // tn),
        in_specs=[
            pl.BlockSpec((tm, tn), lambda i, j: (i, j)),
            pl.BlockSpec((tm, k), lambda i, j: (i, 0)),
            pl.BlockSpec((None, k, tn), lambda i, j: (layer, 0, j)),
        ],
        out_specs=pl.BlockSpec((tm, tn), lambda i, j: (i, j)),
        out_shape=jax.ShapeDtypeStruct((m, d), F32),
        compiler_params=_params(("parallel", "arbitrary")),
        name="ffn_down",
    )(x2, act, w_down)


def _rmsnorm_kernel(x_ref, g_ref, o_ref):
    o_ref[...] = _rms_rows(x_ref[...], g_ref[...])


def _rmsnorm(x2, g):
    m, d = x2.shape
    tm = min(MM_TILE_M, m)
    return pl.pallas_call(
        _rmsnorm_kernel,
        grid=(m // tm,),
        in_specs=[pl.BlockSpec((tm, d), lambda i: (i, 0)), pl.BlockSpec((1, d), lambda i: (0, 0))],
        out_specs=pl.BlockSpec((tm, d), lambda i: (i, 0)),
        out_shape=jax.ShapeDtypeStruct((m, d), F32),
        compiler_params=_params(("parallel",)),
        name="final_rmsnorm",
    )(x2, g)


def _rotary_tables(t):
    half = HEAD_K // 2
    inv = 1.0 / (ROPE_BASE ** jnp.linspace(0.0, 1.0, half, dtype=F32))
    ang = jnp.arange(t, dtype=jnp.int32).astype(F32)[:, None] * inv[None, :]
    cos, sin = jnp.cos(ang), jnp.sin(ang)
    cos_l = jnp.tile(cos, (1, 2 * PAIR_HEADS))
    sin_l = jnp.tile(jnp.concatenate([-sin, sin], axis=-1), (1, PAIR_HEADS))
    return cos_l, sin_l


def kernel(x, ln1_g, w_in, diff_lambda, rel_bias, gla_gate_w, gla_gate_b, ret_decay_logit, head_gain,
           w_o, ln2_g, w_up, conv_w, conv_b, w_down, final_g):
    batch, t, d = x.shape
    depth = w_in.shape[0]
    diff_w = DIFF_HEADS * HEAD_V
    gla_w = 2 * N_PAIRS * HEAD_V
    gla_k = 2 * N_PAIRS * HEAD_K
    ga0 = 3 * diff_w + 2 * gla_k + 2 * gla_w
    ga1 = ga0 + 2 * GLA_GATE_RANK
    gla_col0 = 3 * diff_w // LANES
    ret_col0 = ga0 // LANES

    w_main = jnp.concatenate([w_in[:, :, :ga0], w_in[:, :, ga1:]], axis=-1).astype(BF16)
    w_ga = jnp.pad(w_in[:, :, ga0:ga1], ((0, 0), (0, 0), (0, LANES - 2 * GLA_GATE_RANK))).astype(BF16)
    w_o16, w_up16, w_down16 = w_o.astype(BF16), w_up.astype(BF16), w_down.astype(BF16)
    k_scale = HEAD_K ** -0.5
    colscale = jnp.ones((1, w_main.shape[-1]), F32)
    colscale = colscale.at[:, :diff_w].set(k_scale * LOG2E)
    colscale = colscale.at[:, 3 * diff_w:3 * diff_w + gla_k].set(k_scale)
    colscale = colscale.at[:, ga0 + gla_k:ga0 + 2 * gla_k].set(k_scale)
    r = GLA_GATE_RANK
    gate_wf = jnp.pad(gla_gate_w[:, 0], ((0, 0), (0, LANES - r), (0, 0)))
    gate_wb = jnp.pad(gla_gate_w[:, 1], ((0, 0), (r, LANES - 2 * r), (0, 0)))
    ret_logit = jnp.repeat(ret_decay_logit, HEAD_K, axis=-1)

    rel = (jnp.arange(ATT_TILE, dtype=jnp.int32)[:, None] - jnp.arange(ATT_TILE, dtype=jnp.int32)[None, :])
    rel = rel[None] + ATT_TILE * (jnp.arange(5, dtype=jnp.int32) - 2)[:, None, None]
    bias_tiles = _bias_tiles(rel_bias.astype(F32), _t5_bucket(rel))
    cos_l, sin_l = _rotary_tables(t)

    x2 = x.reshape(batch * t, d)
    for layer in range(depth):
        lam_init = 0.8 - 0.6 * math.exp(-0.3 * layer)
        gain = head_gain[layer][None, :]
        proj, ga = _inproj(x2, ln1_g[layer][None, :], w_main, w_ga, colscale, layer)
        a_out = _diff_attention(proj, bias_tiles, diff_lambda[layer], gain[:, :diff_w], lam_init, batch, t)
        b_out = _linattn("gla", proj, gla_col0, gain[:, diff_w:diff_w + gla_w],
                         (ga, gate_wf[layer], gate_wb[layer], gla_gate_b[layer]), batch, t)
        c_out = _linattn("ret", proj, ret_col0, gain[:, diff_w + gla_w:],
                         (cos_l, sin_l, ret_logit[layer]), batch, t)
        x2 = _outproj(x2, a_out, b_out, c_out, w_o16, layer)
        act = _ffn_up(x2, ln2_g[layer][None, :], w_up16, conv_w[layer], conv_b[layer][None, :], t, layer)
        x2 = _ffn_down(x2, act, w_down16, layer)
    return _rmsnorm(x2, final_g[None, :]).reshape(batch, t, d)
```

```python
import functools
import math

import jax
import jax.numpy as jnp
from jax import lax
from jax.experimental import pallas as pl
from jax.experimental.pallas import tpu as pltpu

F32 = jnp.float32
BF16 = jnp.bfloat16

HEAD_V = 128
DIFF_HEADS = 8
PAIR_HEADS = 2
N_PAIRS = 2
HEAD_K = 64
GLA_GATE_RANK = 16
GLA_TAU = 16.0
ROPE_BASE = 10000.0
N_BUCKETS = 32
MAX_DISTANCE = 128
CONV_W = 3
EPS = 1e-6
LOG2E = math.log2(math.e)

LANES = 128
SUBLANES = 8
BF16_ROWS = 16
VMEM_LIMIT_BYTES = 56 * 1024 * 1024

ATT_TILE = 256
ATT_QTILE = 1024
ATT_KSTAGE = 1024
GLA_CHUNK = 64
RET_CHUNK = 128
LIN_UNROLL = 4
MM_TILE_M = 1024
MM_TILE_N = 1024
INPROJ_TILE_N = 768
FFN_TILE_N = 512
HALO = SUBLANES

NEG_BIG = -1e30

_NT = (((1,), (1,)), ((), ()))
_TN = (((0,), (0,)), ((), ()))


def _params(sem):
    return pltpu.CompilerParams(dimension_semantics=sem, vmem_limit_bytes=VMEM_LIMIT_BYTES)


def _dot(a, b):
    return jnp.dot(a, b, preferred_element_type=F32)


def _dot_f32(a, b):
    return jnp.dot(a, b, preferred_element_type=F32, precision=lax.Precision.HIGHEST)


def _rms_rows(x, g):
    return x * lax.rsqrt(jnp.mean(x * x, axis=-1, keepdims=True) + EPS) * g


def _log_sigmoid(z):
    return jnp.minimum(z, 0.0) - jnp.log1p(jnp.exp(-jnp.abs(z)))


def _silu(z):
    return z * (1.0 / (1.0 + jnp.exp(-z)))


def _inproj_kernel(x_ref, g_ref, wa_ref, wc_ref, wga_ref, cs_ref, proj_ref, ga_ref, h_scr, *, n_head_tiles):
    j = pl.program_id(1)

    @pl.when(j == 0)
    def _():
        hb = _rms_rows(x_ref[...], g_ref[...]).astype(BF16)
        h_scr[...] = hb
        ga_ref[...] = _dot(hb, wga_ref[...])

    @pl.when(j < n_head_tiles)
    def _():
        proj_ref[...] = (_dot(h_scr[...], wa_ref[...]) * cs_ref[...]).astype(BF16)

    @pl.when(j >= n_head_tiles)
    def _():
        proj_ref[...] = (_dot(h_scr[...], wc_ref[...]) * cs_ref[...]).astype(BF16)


def _inproj(x2, g, w, w_tail, wga, colscale, head_cols, layer):
    m, d = x2.shape
    n = head_cols + w_tail.shape[-1]
    tm, tn = min(MM_TILE_M, m), INPROJ_TILE_N
    n_head = head_cols // tn
    return pl.pallas_call(
        functools.partial(_inproj_kernel, n_head_tiles=n_head),
        grid=(m // tm, n // tn),
        in_specs=[
            pl.BlockSpec((tm, d), lambda i, j: (i, 0)),
            pl.BlockSpec((1, d), lambda i, j: (0, 0)),
            pl.BlockSpec((None, d, tn), lambda i, j: (layer, 0, jnp.minimum(j, n_head - 1))),
            pl.BlockSpec((None, d, tn), lambda i, j: (layer, 0, jnp.maximum(j - n_head, 0))),
            pl.BlockSpec((None, d, LANES), lambda i, j: (layer, 0, 0)),
            pl.BlockSpec((1, tn), lambda i, j: (0, j)),
        ],
        out_specs=[
            pl.BlockSpec((tm, tn), lambda i, j: (i, j)),
            pl.BlockSpec((tm, LANES), lambda i, j: (i, 0)),
        ],
        out_shape=[jax.ShapeDtypeStruct((m, n), BF16), jax.ShapeDtypeStruct((m, LANES), F32)],
        scratch_shapes=[pltpu.VMEM((tm, d), BF16)],
        compiler_params=_params(("parallel", "arbitrary")),
        name="inproj",
    )(x2, g, w, w_tail, wga, colscale)


def _bias_tiles_kernel(rb_ref, bucket_ref, out_ref):
    h = pl.program_id(0)
    bucket = bucket_ref[...]
    acc = jnp.zeros(bucket.shape, F32)
    for b in range(N_BUCKETS):
        acc = jnp.where(bucket == b, rb_ref[b, h], acc)
    out_ref[...] = acc * LOG2E


def _bias_tiles(rel_bias, bucket):
    nd = bucket.shape[0]
    return pl.pallas_call(
        _bias_tiles_kernel,
        grid=(DIFF_HEADS,),
        in_specs=[
            pl.BlockSpec(memory_space=pltpu.SMEM),
            pl.BlockSpec((nd, ATT_TILE, ATT_TILE), lambda h: (0, 0, 0)),
        ],
        out_specs=pl.BlockSpec((None, nd, ATT_TILE, ATT_TILE), lambda h: (h, 0, 0, 0)),
        out_shape=jax.ShapeDtypeStruct((DIFF_HEADS, nd, ATT_TILE, ATT_TILE), F32),
        compiler_params=_params(("arbitrary",)),
        name="bias_tiles",
    )(rel_bias, bucket)


def _t5_bucket(rel):
    half = N_BUCKETS // 2
    max_exact = half // 2
    ret = jnp.where(rel > 0, half, 0)
    n = jnp.abs(rel)
    nf = jnp.maximum(n, 1).astype(F32)
    large = max_exact + (jnp.log(nf / max_exact) / math.log(MAX_DISTANCE / max_exact)
                         * (half - max_exact)).astype(jnp.int32)
    large = jnp.minimum(large, half - 1)
    return ret + jnp.where(n < max_exact, n, large)


ACC_ROWS = HEAD_V + BF16_ROWS


def _attn_kernel(q_ref, k_ref, v_ref, bias_ref, lam_ref, gain_ref, o_ref,
                 vt_scr, qh_scr, s_scr, p_scr, mloc_scr, alpha_scr, m_scr, acc_scr,
                 *, lam_init, n_stages, stage_keys, q_tile_rows):
    tile = ATT_TILE
    tq, ks = q_tile_rows, stage_keys
    n_qb = tq // tile
    n_kb = ks // tile
    ones_rows = (lax.broadcasted_iota(jnp.int32, (BF16_ROWS, ks), 0) == 0).astype(BF16)
    for c in range(n_stages):
        for kb in range(n_kb):
            vc = v_ref[c * ks + kb * tile:c * ks + (kb + 1) * tile, :].astype(F32)
            vt_scr[c, 0:HEAD_V, kb * tile:(kb + 1) * tile] = vc.T.astype(BF16)
        vt_scr[c, HEAD_V:ACC_ROWS, :] = ones_rows

    lp = lam_ref[...]
    lam = (jnp.exp(jnp.sum(lp[0:1] * lp[1:2], axis=-1, keepdims=True))
           - jnp.exp(jnp.sum(lp[2:3] * lp[3:4], axis=-1, keepdims=True)) + lam_init)
    first_half = lax.broadcasted_iota(jnp.int32, (1, 2 * HEAD_K), 1) < HEAD_K
    gain = gain_ref[...] * (1.0 - lam_init)

    def q_tile(qi, _):
        rows = pl.ds(pl.multiple_of(qi * tq, tq), tq)
        q = q_ref[rows, :]
        zero = jnp.zeros_like(q)
        qh_scr[0] = jnp.where(first_half, q, zero)
        qh_scr[1] = jnp.where(first_half, zero, q)
        for half in range(2):
            m_scr[half] = jnp.full((1, tq), NEG_BIG, F32)
            acc_scr[half, :, 0:tq] = jnp.zeros((ACC_ROWS, tq), F32)

        def stage_a(c):
            kc = k_ref[pl.ds(pl.multiple_of(c * ks, ks), ks), :]
            for qb in range(n_qb):
                cols = slice(qb * tile, (qb + 1) * tile)
                s = [lax.dot_general(kc, qh_scr[half, cols, :], _NT, preferred_element_type=F32)
                     for half in range(2)]
                mloc = [None, None]
                for kb in range(n_kb):
                    krows = slice(kb * tile, (kb + 1) * tile)
                    bias = bias_ref[jnp.clip((c * n_kb + kb) - (qi * n_qb + qb), -2, 2) + 2]
                    for half in range(2):
                        sb = s[half][krows] + bias
                        s_scr[half, krows, cols] = sb
                        bmax = jnp.max(sb, axis=0, keepdims=True)
                        mloc[half] = bmax if kb == 0 else jnp.maximum(mloc[half], bmax)
                for half in range(2):
                    mloc_scr[half, :, cols] = mloc[half]

        def stage_b():
            for half in range(2):
                m_old = m_scr[half]
                m_new = jnp.maximum(m_old, mloc_scr[half])
                alpha_scr[half] = jnp.exp2(m_old - m_new)
                m_scr[half] = m_new
                p_scr[half, :, 0:tq] = jnp.exp2(s_scr[half, :, 0:tq] - m_new).astype(BF16)

        def stage_c(c):
            vt = vt_scr[c]
            for half in range(2):
                acc_scr[half, :, 0:tq] = (acc_scr[half, :, 0:tq] * alpha_scr[half]
                                          + _dot(vt, p_scr[half, :, 0:tq]))

        stage_a(0)
        stage_b()
        stage_a(1)

        def body(c, _):
            stage_c(c)
            stage_b()
            stage_a(c + 2)
            return 0

        lax.fori_loop(0, n_stages - 2, body, 0)
        stage_c(n_stages - 2)
        stage_b()
        stage_c(n_stages - 1)

        acc1, acc2 = acc_scr[0, :, 0:tq], acc_scr[1, :, 0:tq]
        o1 = acc1[0:HEAD_V] / acc1[HEAD_V:HEAD_V + 1]
        o2 = acc2[0:HEAD_V] / acc2[HEAD_V:HEAD_V + 1]
        ot = o1 - lam * o2
        ot = ot * lax.rsqrt(jnp.mean(ot * ot, axis=0, keepdims=True) + EPS)
        o_ref[rows, :] = (ot.T * gain).astype(BF16)
        return 0

    lax.fori_loop(0, (n_stages * ks) // tq, q_tile, 0)


def _diff_attention(proj, bias_tiles, lam_params, gain, lam_init, batch, t):
    nd = bias_tiles.shape[1]
    tile = ATT_TILE
    tq = min(ATT_QTILE, t)
    ks = min(ATT_KSTAGE, t // 2)
    n_stages = t // ks
    return pl.pallas_call(
        functools.partial(_attn_kernel, lam_init=lam_init, n_stages=n_stages, stage_keys=ks, q_tile_rows=tq),
        grid=(batch, DIFF_HEADS),
        in_specs=[
            pl.BlockSpec((t, HEAD_V), lambda b, h: (b, h)),
            pl.BlockSpec((t, HEAD_V), lambda b, h: (b, DIFF_HEADS + h)),
            pl.BlockSpec((t, HEAD_V), lambda b, h: (b, 2 * DIFF_HEADS + h)),
            pl.BlockSpec((None, nd, tile, tile), lambda b, h: (h, 0, 0, 0)),
            pl.BlockSpec(lam_params.shape, lambda b, h: (0, 0)),
            pl.BlockSpec((1, HEAD_V), lambda b, h: (0, h)),
        ],
        out_specs=pl.BlockSpec((t, HEAD_V), lambda b, h: (b, h)),
        out_shape=jax.ShapeDtypeStruct((batch * t, DIFF_HEADS * HEAD_V), BF16),
        scratch_shapes=[
            pltpu.VMEM((n_stages, ACC_ROWS, ks), BF16),
            pltpu.VMEM((2, tq, 2 * HEAD_K), BF16),
            pltpu.VMEM((2, ks, tq + LANES), F32),
            pltpu.VMEM((2, ks, tq + LANES), BF16),
            pltpu.VMEM((2, 1, tq), F32),
            pltpu.VMEM((2, 1, tq), F32),
            pltpu.VMEM((2, 1, tq), F32),
            pltpu.VMEM((2, ACC_ROWS, tq + LANES), F32),
        ],
        compiler_params=_params(("parallel", "arbitrary")),
        name="diff_attn",
    )(proj, proj, proj, bias_tiles, lam_params, gain)


def _linattn_kernel(*refs, mode, chunk, n_chunks):
    if mode == "gla":
        (q_ref, k_ref, v_ref, gate_ref, gain_ref, ga_ref, wf_ref, wb_ref, gb_ref,
         o_ref, b_scr, einc_scr, lab_scr, kvf_scr, rnext_scr) = refs
    else:
        (q_ref, k_ref, v_ref, gate_ref, gain_ref, cos_ref, sin_ref, logit_ref,
         o_ref, kx_scr, kvf_scr, rnext_scr) = refs
    c = chunk
    kw = PAIR_HEADS * HEAD_K
    vw = PAIR_HEADS * HEAD_V

    lane_k = lax.broadcasted_iota(jnp.int32, (1, kw), 1)
    head0_k = lane_k < HEAD_K
    st_row = lax.broadcasted_iota(jnp.int32, (vw, kw), 0)
    st_lane = lax.broadcasted_iota(jnp.int32, (vw, kw), 1)
    same_head = (st_row < HEAD_V) == (st_lane < HEAD_K)
    ri = lax.broadcasted_iota(jnp.int32, (c, c), 0)
    ci = lax.broadcasted_iota(jnp.int32, (c, c), 1)
    lower, upper = ri >= ci, ri <= ci

    def chunk_rows(n):
        return pl.ds(pl.multiple_of(n * c, c), c)

    if mode == "ret":
        pr = lax.broadcasted_iota(jnp.int32, (kw, kw), 0)
        pc = lax.broadcasted_iota(jnp.int32, (kw, kw), 1)
        swap = (pr == (pc ^ (HEAD_K // 2))).astype(BF16)

        def rotary(x_bf16, rows):
            xr = _dot(x_bf16, swap)
            return x_bf16.astype(F32) * cos_ref[rows, :] + xr * sin_ref[rows, :]

        lg = _log_sigmoid(logit_ref[...])
        lgf, lgb = lg[0:1], lg[1:2]
        pos = lax.broadcasted_iota(jnp.int32, (c, kw), 0).astype(F32)
        b_const, e_const = (pos + 1.0) * lgf, pos * lgb
        totf_const, totb_const = float(c) * lgf, float(c) * lgb
        rel = (ri - ci).astype(F32)
        dmask = []
        for h in range(PAIR_HEADS):
            lf = lgf[:, h * HEAD_K:h * HEAD_K + 1]
            lb = lgb[:, h * HEAD_K:h * HEAD_K + 1]
            dmask.append(jnp.where(lower, jnp.exp(lf * jnp.maximum(rel, 0.0)), 0.0)
                         + jnp.where(upper, jnp.exp(lb * jnp.maximum(-rel, 0.0)), 0.0))
    else:
        t_rows = n_chunks * c
        in_chunk = lax.broadcasted_iota(jnp.int32, (t_rows, kw), 0) & (c - 1)

        def chunk_cumsum(x):
            shift = 1
            while shift < c:
                x = x + jnp.where(in_chunk >= shift, pltpu.roll(x, shift, 0), 0.0)
                shift *= 2
            return x

        ga = ga_ref[...]
        laf = _log_sigmoid(_dot_f32(ga, wf_ref[...]) + gb_ref[0:1, :]) * (1.0 / GLA_TAU)
        lab = _log_sigmoid(_dot_f32(ga, wb_ref[...]) + gb_ref[1:2, :]) * (1.0 / GLA_TAU)
        b_scr[...] = chunk_cumsum(laf)
        einc_scr[...] = chunk_cumsum(lab)
        lab_scr[...] = lab

    def gla_decays(rows):
        b, einc = b_scr[rows, :], einc_scr[rows, :]
        return b, einc - lab_scr[rows, :], b[c - 1:c, :], einc[c - 1:c, :]

    def phase1(i, r_state):
        n = n_chunks - 1 - i
        rows = chunk_rows(n)
        v = v_ref[rows, :]
        if mode == "gla":
            k = k_ref[rows, :].astype(F32)
            b, e, totf, totb = gla_decays(rows)
        else:
            k = rotary(k_ref[rows, :], rows)
            kx_scr[rows, :] = k
            b, e, totf, totb = b_const, e_const, totf_const, totb_const
        kf = (k * jnp.exp(totf - b)).astype(BF16)
        kb = (k * jnp.exp(e)).astype(BF16)
        kvf = lax.dot_general(v, kf, _TN, preferred_element_type=F32)
        kvb = lax.dot_general(v, kb, _TN, preferred_element_type=F32)
        kvf_scr[n] = jnp.where(same_head, kvf, 0.0)
        rnext_scr[n] = r_state.astype(BF16)
        return jnp.exp(totb) * r_state + jnp.where(same_head, kvb, 0.0)

    lax.fori_loop(0, n_chunks, phase1, jnp.zeros((vw, kw), F32), unroll=LIN_UNROLL)

    def phase2(n, s_state):
        rows = chunk_rows(n)
        v = v_ref[rows, :]
        if mode == "gla":
            q = q_ref[rows, :].astype(F32)
            k = k_ref[rows, :].astype(F32)
            b, e, totf, totb = gla_decays(rows)
        else:
            q = rotary(q_ref[rows, :], rows)
            k = kx_scr[rows, :]
            b, e, totf, totb = b_const, e_const, totf_const, totb_const

        qf = (q * jnp.exp(b)).astype(BF16)
        qb = (q * jnp.exp(totb - e)).astype(BF16)
        inter = (lax.dot_general(qf, s_state.astype(BF16), _NT, preferred_element_type=F32)
                 + lax.dot_general(qb, rnext_scr[n], _NT, preferred_element_type=F32))

        if mode == "gla":
            bm = b[c // 2 - 1:c // 2, :]
            em = e[c // 2:c // 2 + 1, :]
            qtf, ktf = q * jnp.exp(b - bm), (k * jnp.exp(bm - b)).astype(BF16)
            qtb, ktb = q * jnp.exp(em - e), (k * jnp.exp(e - em)).astype(BF16)
        else:
            kb16 = k.astype(BF16)
        outs = []
        for h in range(PAIR_HEADS):
            mh = head0_k if h == 0 else jnp.logical_not(head0_k)
            if mode == "gla":
                sf = lax.dot_general(jnp.where(mh, qtf, 0.0).astype(BF16), ktf, _NT, preferred_element_type=F32)
                sb = lax.dot_general(jnp.where(mh, qtb, 0.0).astype(BF16), ktb, _NT, preferred_element_type=F32)
                sc = jnp.where(lower, sf, 0.0) + jnp.where(upper, sb, 0.0)
            else:
                sc = lax.dot_general(jnp.where(mh, q, 0.0).astype(BF16), kb16, _NT,
                                     preferred_element_type=F32) * dmask[h]
            outs.append(_dot(sc.astype(BF16), v[:, h * HEAD_V:(h + 1) * HEAD_V]))

        gate = gate_ref[rows, :].astype(F32)
        for h in range(PAIR_HEADS):
            cols = slice(h * HEAD_V, (h + 1) * HEAD_V)
            o = outs[h] + inter[:, cols]
            o = _rms_rows(o, gain_ref[:, cols]) * _silu(gate[:, cols])
            o_ref[rows, cols] = o.astype(BF16)

        kvf = kvf_scr[n]
        return jnp.exp(totf) * s_state + kvf

    lax.fori_loop(0, n_chunks, phase2, jnp.zeros((vw, kw), F32), unroll=LIN_UNROLL)


def _linattn(mode, proj, col0, gain, extra, batch, t):
    chunk = GLA_CHUNK if mode == "gla" else RET_CHUNK
    n_chunks = t // chunk
    kw, vw = PAIR_HEADS * HEAD_K, PAIR_HEADS * HEAD_V
    v0 = (col0 + 2 * N_PAIRS) // 2
    g0 = v0 + N_PAIRS
    in_specs = [
        pl.BlockSpec((t, kw), lambda b, p: (b, col0 + p)),
        pl.BlockSpec((t, kw), lambda b, p: (b, col0 + N_PAIRS + p)),
        pl.BlockSpec((t, vw), lambda b, p: (b, v0 + p)),
        pl.BlockSpec((t, vw), lambda b, p: (b, g0 + p)),
        pl.BlockSpec((1, vw), lambda b, p: (0, p)),
    ]
    args = [proj, proj, proj, proj, gain]
    state_scratch = [pltpu.VMEM((n_chunks, vw, kw), F32), pltpu.VMEM((n_chunks, vw, kw), BF16)]
    if mode == "gla":
        ga, wf, wb, gb = extra
        in_specs += [
            pl.BlockSpec((t, LANES), lambda b, p: (b, 0)),
            pl.BlockSpec((LANES, kw), lambda b, p: (0, p)),
            pl.BlockSpec((LANES, kw), lambda b, p: (0, p)),
            pl.BlockSpec((2, kw), lambda b, p: (0, p)),
        ]
        args += [ga, wf, wb, gb]
        scratch = [pltpu.VMEM((t, kw), F32)] * 3 + state_scratch
    else:
        cos, sin, logit = extra
        in_specs += [
            pl.BlockSpec((t, kw), lambda b, p: (0, 0)),
            pl.BlockSpec((t, kw), lambda b, p: (0, 0)),
            pl.BlockSpec((2, kw), lambda b, p: (0, p)),
        ]
        args += [cos, sin, logit]
        scratch = [pltpu.VMEM((t, kw), F32)] + state_scratch
    return pl.pallas_call(
        functools.partial(_linattn_kernel, mode=mode, chunk=chunk, n_chunks=n_chunks),
        grid=(batch, N_PAIRS),
        in_specs=in_specs,
        out_specs=pl.BlockSpec((t, vw), lambda b, p: (b, p)),
        out_shape=jax.ShapeDtypeStruct((batch * t, N_PAIRS * vw), BF16),
        scratch_shapes=scratch,
        compiler_params=_params(("parallel", "arbitrary")),
        name="linattn_" + mode,
    )(*args)


def _outproj_kernel(x_ref, a_ref, b_ref, c_ref, wa_ref, wb_ref, wc_ref, o_ref):
    o_ref[...] = (x_ref[...] + _dot(a_ref[...], wa_ref[...]) + _dot(b_ref[...], wb_ref[...])
                  + _dot(c_ref[...], wc_ref[...]))


def _outproj(x2, a, b, c, w_o, layer):
    m, d = x2.shape
    tm, tn = min(MM_TILE_M, m), min(MM_TILE_N, d)
    wa_rows, wb_rows = a.shape[1], b.shape[1]
    nb = wa_rows // wb_rows
    return pl.pallas_call(
        _outproj_kernel,
        grid=(m // tm, d // tn),
        in_specs=[
            pl.BlockSpec((tm, tn), lambda i, j: (i, j)),
            pl.BlockSpec((tm, wa_rows), lambda i, j: (i, 0)),
            pl.BlockSpec((tm, wb_rows), lambda i, j: (i, 0)),
            pl.BlockSpec((tm, wb_rows), lambda i, j: (i, 0)),
            pl.BlockSpec((None, wa_rows, tn), lambda i, j: (layer, 0, j)),
            pl.BlockSpec((None, wb_rows, tn), lambda i, j: (layer, nb, j)),
            pl.BlockSpec((None, wb_rows, tn), lambda i, j: (layer, nb + 1, j)),
        ],
        out_specs=pl.BlockSpec((tm, tn), lambda i, j: (i, j)),
        out_shape=jax.ShapeDtypeStruct((m, d), F32),
        compiler_params=_params(("parallel", "arbitrary")),
        name="outproj",
    )(x2, a, b, c, w_o, w_o, w_o)


def _ffn_up_kernel(x_ref, xp_ref, xn_ref, g_ref, wa_ref, wg_ref, cwa_ref, cwg_ref, cba_ref, cbg_ref,
                   o_ref, h_scr, *, tiles_per_seq):
    i = pl.program_id(0)
    tm = x_ref.shape[0]

    @pl.when(pl.program_id(1) == 0)
    def _():
        g = g_ref[...]
        keep_prev = (i % tiles_per_seq != 0).astype(F32)
        keep_next = ((i + 1) % tiles_per_seq != 0).astype(F32)
        h_scr[0:HALO, :] = (_rms_rows(xp_ref[...], g) * keep_prev).astype(BF16)
        h_scr[HALO:HALO + tm, :] = _rms_rows(x_ref[...], g).astype(BF16)
        h_scr[HALO + tm:, :] = (_rms_rows(xn_ref[...], g) * keep_next).astype(BF16)

    h = h_scr[...]
    n_ext = tm + 2 * HALO

    def conv(w_ref, cw_ref, cb_ref):
        u = _dot(h, w_ref[...])
        prev = pltpu.roll(u, 1, 0)[HALO:HALO + tm]
        nxt = pltpu.roll(u, n_ext - 1, 0)[HALO:HALO + tm]
        return prev * cw_ref[0:1, :] + u[HALO:HALO + tm] * cw_ref[1:2, :] + nxt * cw_ref[2:3, :] + cb_ref[...]

    gate = _silu(conv(wg_ref, cwg_ref, cbg_ref))
    o_ref[...] = (conv(wa_ref, cwa_ref, cba_ref) * gate).astype(BF16)


def _ffn_up(x2, g, w_up, conv_w, conv_b, t, layer):
    m, d = x2.shape
    d_ff = w_up.shape[-1] // 2
    tm = min(MM_TILE_M, t)
    tn = FFN_TILE_N
    nj = d_ff // tn
    blocks_per_tile = tm // HALO
    last_block = m // HALO - 1
    return pl.pallas_call(
        functools.partial(_ffn_up_kernel, tiles_per_seq=t // tm),
        grid=(m // tm, nj),
        in_specs=[
            pl.BlockSpec((tm, d), lambda i, j: (i, 0)),
            pl.BlockSpec((HALO, d), lambda i, j: (jnp.maximum(i * blocks_per_tile - 1, 0), 0)),
            pl.BlockSpec((HALO, d), lambda i, j: (jnp.minimum((i + 1) * blocks_per_tile, last_block), 0)),
            pl.BlockSpec((1, d), lambda i, j: (0, 0)),
            pl.BlockSpec((None, d, tn), lambda i, j: (layer, 0, j)),
            pl.BlockSpec((None, d, tn), lambda i, j: (layer, 0, nj + j)),
            pl.BlockSpec((CONV_W, tn), lambda i, j: (0, j)),
            pl.BlockSpec((CONV_W, tn), lambda i, j: (0, nj + j)),
            pl.BlockSpec((1, tn), lambda i, j: (0, j)),
            pl.BlockSpec((1, tn), lambda i, j: (0, nj + j)),
        ],
        out_specs=pl.BlockSpec((tm, tn), lambda i, j: (i, j)),
        out_shape=jax.ShapeDtypeStruct((m, d_ff), BF16),
        scratch_shapes=[pltpu.VMEM((tm + 2 * HALO, d), BF16)],
        compiler_params=_params(("parallel", "arbitrary")),
        name="ffn_up",
    )(x2, x2, x2, g, w_up, w_up, conv_w, conv_w, conv_b, conv_b)


def _ffn_down_kernel(x_ref, a_ref, w_ref, o_ref):
    o_ref[...] = x_ref[...] + _dot(a_ref[...], w_ref[...])


def _ffn_down(x2, act, w_down, layer):
    m, d = x2.shape
    k = act.shape[1]
    tm, tn = min(MM_TILE_M, m), FFN_TILE_N
    return pl.pallas_call(
        _ffn_down_kernel,
        grid=(m // tm, d // tn),
        in_specs=[
            pl.BlockSpec((tm, tn), lambda i, j: (i, j)),
            pl.BlockSpec((tm, k), lambda i, j: (i, 0)),
            pl.BlockSpec((None, k, tn), lambda i, j: (layer, 0, j)),
        ],
        out_specs=pl.BlockSpec((tm, tn), lambda i, j: (i, j)),
        out_shape=jax.ShapeDtypeStruct((m, d), F32),
        compiler_params=_params(("parallel", "arbitrary")),
        name="ffn_down",
    )(x2, act, w_down)


def _rmsnorm_kernel(x_ref, g_ref, o_ref):
    o_ref[...] = _rms_rows(x_ref[...], g_ref[...])


def _rmsnorm(x2, g):
    m, d = x2.shape
    tm = min(MM_TILE_M, m)
    return pl.pallas_call(
        _rmsnorm_kernel,
        grid=(m // tm,),
        in_specs=[pl.BlockSpec((tm, d), lambda i: (i, 0)), pl.BlockSpec((1, d), lambda i: (0, 0))],
        out_specs=pl.BlockSpec((tm, d), lambda i: (i, 0)),
        out_shape=jax.ShapeDtypeStruct((m, d), F32),
        compiler_params=_params(("parallel",)),
        name="final_rmsnorm",
    )(x2, g)


def _rotary_tables(t):
    half = HEAD_K // 2
    inv = 1.0 / (ROPE_BASE ** jnp.linspace(0.0, 1.0, half, dtype=F32))
    ang = jnp.arange(t, dtype=jnp.int32).astype(F32)[:, None] * inv[None, :]
    cos, sin = jnp.cos(ang), jnp.sin(ang)
    cos_l = jnp.tile(cos, (1, 2 * PAIR_HEADS))
    sin_l = jnp.tile(jnp.concatenate([-sin, sin], axis=-1), (1, PAIR_HEADS))
    return cos_l, sin_l


def kernel(x, ln1_g, w_in, diff_lambda, rel_bias, gla_gate_w, gla_gate_b, ret_decay_logit, head_gain,
           w_o, ln2_g, w_up, conv_w, conv_b, w_down, final_g):
    batch, t, d = x.shape
    depth = w_in.shape[0]
    diff_w = DIFF_HEADS * HEAD_V
    gla_w = 2 * N_PAIRS * HEAD_V
    gla_k = 2 * N_PAIRS * HEAD_K
    ga0 = 3 * diff_w + 2 * gla_k + 2 * gla_w
    ga1 = ga0 + 2 * GLA_GATE_RANK
    gla_col0 = 3 * diff_w // LANES
    ret_col0 = ga0 // LANES

    w_in16 = w_in.astype(BF16)
    w_tail = w_in16[:, :, ga1:]
    w_ga = jnp.pad(w_in16[:, :, ga0:ga1], ((0, 0), (0, 0), (0, LANES - 2 * GLA_GATE_RANK)))
    n_proj = ga0 + w_tail.shape[-1]
    w_o16, w_up16, w_down16 = w_o.astype(BF16), w_up.astype(BF16), w_down.astype(BF16)
    k_scale = HEAD_K ** -0.5
    colscale = jnp.ones((1, n_proj), F32)
    colscale = colscale.at[:, :diff_w].set(k_scale * LOG2E)
    colscale = colscale.at[:, 3 * diff_w:3 * diff_w + gla_k].set(k_scale)
    colscale = colscale.at[:, ga0 + gla_k:ga0 + 2 * gla_k].set(k_scale)
    r = GLA_GATE_RANK
    gate_wf = jnp.pad(gla_gate_w[:, 0], ((0, 0), (0, LANES - r), (0, 0)))
    gate_wb = jnp.pad(gla_gate_w[:, 1], ((0, 0), (r, LANES - 2 * r), (0, 0)))
    ret_logit = jnp.repeat(ret_decay_logit, HEAD_K, axis=-1)

    rel = (jnp.arange(ATT_TILE, dtype=jnp.int32)[:, None] - jnp.arange(ATT_TILE, dtype=jnp.int32)[None, :])
    rel = rel[None] + ATT_TILE * (jnp.arange(5, dtype=jnp.int32) - 2)[:, None, None]
    bias_tiles = _bias_tiles(rel_bias.astype(F32), _t5_bucket(rel))
    cos_l, sin_l = _rotary_tables(t)

    x2 = x.reshape(batch * t, d)
    for layer in range(depth):
        lam_init = 0.8 - 0.6 * math.exp(-0.3 * layer)
        gain = head_gain[layer][None, :]
        proj, ga = _inproj(x2, ln1_g[layer][None, :], w_in16, w_tail, w_ga, colscale, ga0, layer)
        a_out = _diff_attention(proj, bias_tiles, diff_lambda[layer], gain[:, :diff_w], lam_init, batch, t)
        b_out = _linattn("gla", proj, gla_col0, gain[:, diff_w:diff_w + gla_w],
                         (ga, gate_wf[layer], gate_wb[layer], gla_gate_b[layer]), batch, t)
        c_out = _linattn("ret", proj, ret_col0, gain[:, diff_w + gla_w:],
                         (cos_l, sin_l, ret_logit[layer]), batch, t)
        x2 = _outproj(x2, a_out, b_out, c_out, w_o16, layer)
        act = _ffn_up(x2, ln2_g[layer][None, :], w_up16, conv_w[layer], conv_b[layer][None, :], t, layer)
        x2 = _ffn_down(x2, act, w_down16, layer)
    return _rmsnorm(x2, final_g[None, :]).reshape(batch, t, d)
```

```python
import functools
import math

import jax
import jax.numpy as jnp
from jax import lax
from jax.experimental import pallas as pl
from jax.experimental.pallas import tpu as pltpu

F32 = jnp.float32
BF16 = jnp.bfloat16

HEAD_V = 128
DIFF_HEADS = 8
PAIR_HEADS = 2
N_PAIRS = 2
HEAD_K = 64
GLA_GATE_RANK = 16
GLA_TAU = 16.0
ROPE_BASE = 10000.0
N_BUCKETS = 32
MAX_DISTANCE = 128
CONV_W = 3
EPS = 1e-6
LOG2E = math.log2(math.e)

LANES = 128
SUBLANES = 8
BF16_ROWS = 16
VMEM_LIMIT_BYTES = 56 * 1024 * 1024

ATT_TILE = 256
ATT_QTILE = 1024
ATT_KSTAGE = 1024
GLA_CHUNK = 64
RET_CHUNK = 128
LIN_UNROLL = 4
MM_TILE_M = 1024
MM_TILE_N = 1024
FFN_TILE_N = 512
FFN_DOWN_TILE_N = 256
HALO = SUBLANES

NEG_BIG = -1e30

_NT = (((1,), (1,)), ((), ()))
_TN = (((0,), (0,)), ((), ()))


def _params(sem):
    return pltpu.CompilerParams(dimension_semantics=sem, vmem_limit_bytes=VMEM_LIMIT_BYTES)


def _dot(a, b):
    return jnp.dot(a, b, preferred_element_type=F32)


def _dot_f32(a, b):
    return jnp.dot(a, b, preferred_element_type=F32, precision=lax.Precision.HIGHEST)


def _rms_rows(x, g):
    return x * lax.rsqrt(jnp.mean(x * x, axis=-1, keepdims=True) + EPS) * g


def _log_sigmoid(z):
    return jnp.minimum(z, 0.0) - jnp.log1p(jnp.exp(-jnp.abs(z)))


def _silu(z):
    return z * (1.0 / (1.0 + jnp.exp(-z)))


def _inproj_kernel(x_ref, g_ref, w_ref, wga_ref, cs_ref, proj_ref, ga_ref, h_scr):
    @pl.when(pl.program_id(1) == 0)
    def _():
        hb = _rms_rows(x_ref[...], g_ref[...]).astype(BF16)
        h_scr[...] = hb
        ga_ref[...] = _dot(hb, wga_ref[...])

    proj_ref[...] = (_dot(h_scr[...], w_ref[...]) * cs_ref[...]).astype(BF16)


def _inproj(x2, g, w, wga, colscale, layer):
    m, d = x2.shape
    n = w.shape[-1]
    tm, tn = min(MM_TILE_M, m), min(MM_TILE_N, n)
    return pl.pallas_call(
        _inproj_kernel,
        grid=(m // tm, n // tn),
        in_specs=[
            pl.BlockSpec((tm, d), lambda i, j: (i, 0)),
            pl.BlockSpec((1, d), lambda i, j: (0, 0)),
            pl.BlockSpec((None, d, tn), lambda i, j: (layer, 0, j)),
            pl.BlockSpec((None, d, LANES), lambda i, j: (layer, 0, 0)),
            pl.BlockSpec((1, tn), lambda i, j: (0, j)),
        ],
        out_specs=[
            pl.BlockSpec((tm, tn), lambda i, j: (i, j)),
            pl.BlockSpec((tm, LANES), lambda i, j: (i, 0)),
        ],
        out_shape=[jax.ShapeDtypeStruct((m, n), BF16), jax.ShapeDtypeStruct((m, LANES), F32)],
        scratch_shapes=[pltpu.VMEM((tm, d), BF16)],
        compiler_params=_params(("parallel", "arbitrary")),
        name="inproj",
    )(x2, g, w, wga, colscale)


def _bias_tiles_kernel(rb_ref, bucket_ref, out_ref):
    h = pl.program_id(0)
    bucket = bucket_ref[...]
    acc = jnp.zeros(bucket.shape, F32)
    for b in range(N_BUCKETS):
        acc = jnp.where(bucket == b, rb_ref[b, h], acc)
    out_ref[...] = acc * LOG2E


def _bias_tiles(rel_bias, bucket):
    nd = bucket.shape[0]
    return pl.pallas_call(
        _bias_tiles_kernel,
        grid=(DIFF_HEADS,),
        in_specs=[
            pl.BlockSpec(memory_space=pltpu.SMEM),
            pl.BlockSpec((nd, ATT_TILE, ATT_TILE), lambda h: (0, 0, 0)),
        ],
        out_specs=pl.BlockSpec((None, nd, ATT_TILE, ATT_TILE), lambda h: (h, 0, 0, 0)),
        out_shape=jax.ShapeDtypeStruct((DIFF_HEADS, nd, ATT_TILE, ATT_TILE), F32),
        compiler_params=_params(("arbitrary",)),
        name="bias_tiles",
    )(rel_bias, bucket)


def _t5_bucket(rel):
    half = N_BUCKETS // 2
    max_exact = half // 2
    ret = jnp.where(rel > 0, half, 0)
    n = jnp.abs(rel)
    nf = jnp.maximum(n, 1).astype(F32)
    large = max_exact + (jnp.log(nf / max_exact) / math.log(MAX_DISTANCE / max_exact)
                         * (half - max_exact)).astype(jnp.int32)
    large = jnp.minimum(large, half - 1)
    return ret + jnp.where(n < max_exact, n, large)


ACC_ROWS = HEAD_V + BF16_ROWS


def _attn_kernel(q_ref, k_ref, v_ref, bias_ref, lam_ref, gain_ref, o_ref,
                 vt_scr, qh_scr, s_scr, p_scr, mloc_scr, alpha_scr, m_scr, acc_scr,
                 *, lam_init, n_stages, stage_keys, q_tile_rows):
    tile = ATT_TILE
    tq, ks = q_tile_rows, stage_keys
    n_qb = tq // tile
    n_kb = ks // tile
    ones_rows = (lax.broadcasted_iota(jnp.int32, (BF16_ROWS, ks), 0) == 0).astype(BF16)
    for c in range(n_stages):
        for kb in range(n_kb):
            vc = v_ref[c * ks + kb * tile:c * ks + (kb + 1) * tile, :].astype(F32)
            vt_scr[c, 0:HEAD_V, kb * tile:(kb + 1) * tile] = vc.T.astype(BF16)
        vt_scr[c, HEAD_V:ACC_ROWS, :] = ones_rows

    lp = lam_ref[...]
    lam = (jnp.exp(jnp.sum(lp[0:1] * lp[1:2], axis=-1, keepdims=True))
           - jnp.exp(jnp.sum(lp[2:3] * lp[3:4], axis=-1, keepdims=True)) + lam_init)
    first_half = lax.broadcasted_iota(jnp.int32, (1, 2 * HEAD_K), 1) < HEAD_K
    gain = gain_ref[...] * (1.0 - lam_init)

    def q_tile(qi, _):
        rows = pl.ds(pl.multiple_of(qi * tq, tq), tq)
        q = q_ref[rows, :]
        zero = jnp.zeros_like(q)
        qh_scr[0] = jnp.where(first_half, q, zero)
        qh_scr[1] = jnp.where(first_half, zero, q)
        for half in range(2):
            m_scr[half] = jnp.full((1, tq), NEG_BIG, F32)
            acc_scr[half, :, 0:tq] = jnp.zeros((ACC_ROWS, tq), F32)

        def stage_a(c):
            kc = k_ref[pl.ds(pl.multiple_of(c * ks, ks), ks), :]
            for qb in range(n_qb):
                cols = slice(qb * tile, (qb + 1) * tile)
                s = [lax.dot_general(kc, qh_scr[half, cols, :], _NT, preferred_element_type=F32)
                     for half in range(2)]
                mloc = [None, None]
                for kb in range(n_kb):
                    krows = slice(kb * tile, (kb + 1) * tile)
                    bias = bias_ref[jnp.clip((c * n_kb + kb) - (qi * n_qb + qb), -2, 2) + 2]
                    for half in range(2):
                        sb = s[half][krows] + bias
                        s_scr[half, krows, cols] = sb
                        bmax = jnp.max(sb, axis=0, keepdims=True)
                        mloc[half] = bmax if kb == 0 else jnp.maximum(mloc[half], bmax)
                for half in range(2):
                    mloc_scr[half, :, cols] = mloc[half]

        def stage_b():
            for half in range(2):
                m_old = m_scr[half]
                m_new = jnp.maximum(m_old, mloc_scr[half])
                alpha_scr[half] = jnp.exp2(m_old - m_new)
                m_scr[half] = m_new
                p_scr[half, :, 0:tq] = jnp.exp2(s_scr[half, :, 0:tq] - m_new).astype(BF16)

        def stage_c(c):
            vt = vt_scr[c]
            for half in range(2):
                acc_scr[half, :, 0:tq] = (acc_scr[half, :, 0:tq] * alpha_scr[half]
                                          + _dot(vt, p_scr[half, :, 0:tq]))

        stage_a(0)
        stage_b()
        stage_a(1)

        def body(c, _):
            stage_c(c)
            stage_b()
            stage_a(c + 2)
            return 0

        lax.fori_loop(0, n_stages - 2, body, 0)
        stage_c(n_stages - 2)
        stage_b()
        stage_c(n_stages - 1)

        acc1, acc2 = acc_scr[0, :, 0:tq], acc_scr[1, :, 0:tq]
        o1 = acc1[0:HEAD_V] / acc1[HEAD_V:HEAD_V + 1]
        o2 = acc2[0:HEAD_V] / acc2[HEAD_V:HEAD_V + 1]
        ot = o1 - lam * o2
        ot = ot * lax.rsqrt(jnp.mean(ot * ot, axis=0, keepdims=True) + EPS)
        o_ref[rows, :] = (ot.T * gain).astype(BF16)
        return 0

    lax.fori_loop(0, (n_stages * ks) // tq, q_tile, 0)


def _diff_attention(proj, bias_tiles, lam_params, gain, lam_init, batch, t):
    nd = bias_tiles.shape[1]
    tile = ATT_TILE
    tq = min(ATT_QTILE, t)
    ks = min(ATT_KSTAGE, t // 2)
    n_stages = t // ks
    return pl.pallas_call(
        functools.partial(_attn_kernel, lam_init=lam_init, n_stages=n_stages, stage_keys=ks, q_tile_rows=tq),
        grid=(batch, DIFF_HEADS),
        in_specs=[
            pl.BlockSpec((t, HEAD_V), lambda b, h: (b, h)),
            pl.BlockSpec((t, HEAD_V), lambda b, h: (b, DIFF_HEADS + h)),
            pl.BlockSpec((t, HEAD_V), lambda b, h: (b, 2 * DIFF_HEADS + h)),
            pl.BlockSpec((None, nd, tile, tile), lambda b, h: (h, 0, 0, 0)),
            pl.BlockSpec(lam_params.shape, lambda b, h: (0, 0)),
            pl.BlockSpec((1, HEAD_V), lambda b, h: (0, h)),
        ],
        out_specs=pl.BlockSpec((t, HEAD_V), lambda b, h: (b, h)),
        out_shape=jax.ShapeDtypeStruct((batch * t, DIFF_HEADS * HEAD_V), BF16),
        scratch_shapes=[
            pltpu.VMEM((n_stages, ACC_ROWS, ks), BF16),
            pltpu.VMEM((2, tq, 2 * HEAD_K), BF16),
            pltpu.VMEM((2, ks, tq + LANES), F32),
            pltpu.VMEM((2, ks, tq + LANES), BF16),
            pltpu.VMEM((2, 1, tq), F32),
            pltpu.VMEM((2, 1, tq), F32),
            pltpu.VMEM((2, 1, tq), F32),
            pltpu.VMEM((2, ACC_ROWS, tq + LANES), F32),
        ],
        compiler_params=_params(("parallel", "arbitrary")),
        name="diff_attn",
    )(proj, proj, proj, bias_tiles, lam_params, gain)


def _linattn_kernel(*refs, mode, chunk, n_chunks):
    if mode == "gla":
        (q_ref, k_ref, v_ref, gate_ref, gain_ref, ga_ref, wf_ref, wb_ref, gb_ref,
         o_ref, b_scr, einc_scr, lab_scr, kvf_scr, rnext_scr) = refs
    else:
        (q_ref, k_ref, v_ref, gate_ref, gain_ref, cos_ref, sin_ref, logit_ref,
         o_ref, kx_scr, kvf_scr, rnext_scr) = refs
    c = chunk
    kw = PAIR_HEADS * HEAD_K
    vw = PAIR_HEADS * HEAD_V

    lane_k = lax.broadcasted_iota(jnp.int32, (1, kw), 1)
    head0_k = lane_k < HEAD_K
    st_row = lax.broadcasted_iota(jnp.int32, (vw, kw), 0)
    st_lane = lax.broadcasted_iota(jnp.int32, (vw, kw), 1)
    same_head = (st_row < HEAD_V) == (st_lane < HEAD_K)
    ri = lax.broadcasted_iota(jnp.int32, (c, c), 0)
    ci = lax.broadcasted_iota(jnp.int32, (c, c), 1)
    lower, upper = ri >= ci, ri <= ci

    def chunk_rows(n):
        return pl.ds(pl.multiple_of(n * c, c), c)

    if mode == "ret":
        pr = lax.broadcasted_iota(jnp.int32, (kw, kw), 0)
        pc = lax.broadcasted_iota(jnp.int32, (kw, kw), 1)
        swap = (pr == (pc ^ (HEAD_K // 2))).astype(BF16)

        def rotary(x_bf16, rows):
            xr = _dot(x_bf16, swap)
            return x_bf16.astype(F32) * cos_ref[rows, :] + xr * sin_ref[rows, :]

        lg = _log_sigmoid(logit_ref[...])
        lgf, lgb = lg[0:1], lg[1:2]
        pos = lax.broadcasted_iota(jnp.int32, (c, kw), 0).astype(F32)
        b_const, e_const = (pos + 1.0) * lgf, pos * lgb
        totf_const, totb_const = float(c) * lgf, float(c) * lgb
        rel = (ri - ci).astype(F32)
        dmask = []
        for h in range(PAIR_HEADS):
            lf = lgf[:, h * HEAD_K:h * HEAD_K + 1]
            lb = lgb[:, h * HEAD_K:h * HEAD_K + 1]
            dmask.append(jnp.where(lower, jnp.exp(lf * jnp.maximum(rel, 0.0)), 0.0)
                         + jnp.where(upper, jnp.exp(lb * jnp.maximum(-rel, 0.0)), 0.0))
    else:
        t_rows = n_chunks * c
        in_chunk = lax.broadcasted_iota(jnp.int32, (t_rows, kw), 0) & (c - 1)

        def chunk_cumsum(x):
            shift = 1
            while shift < c:
                x = x + jnp.where(in_chunk >= shift, pltpu.roll(x, shift, 0), 0.0)
                shift *= 2
            return x

        ga = ga_ref[...]
        laf = _log_sigmoid(_dot_f32(ga, wf_ref[...]) + gb_ref[0:1, :]) * (1.0 / GLA_TAU)
        lab = _log_sigmoid(_dot_f32(ga, wb_ref[...]) + gb_ref[1:2, :]) * (1.0 / GLA_TAU)
        b_scr[...] = chunk_cumsum(laf)
        einc_scr[...] = chunk_cumsum(lab)
        lab_scr[...] = lab

    def gla_decays(rows):
        b, einc = b_scr[rows, :], einc_scr[rows, :]
        return b, einc - lab_scr[rows, :], b[c - 1:c, :], einc[c - 1:c, :]

    def phase1(i, r_state):
        n = n_chunks - 1 - i
        rows = chunk_rows(n)
        v = v_ref[rows, :]
        if mode == "gla":
            k = k_ref[rows, :].astype(F32)
            b, e, totf, totb = gla_decays(rows)
        else:
            k = rotary(k_ref[rows, :], rows)
            kx_scr[rows, :] = k
            b, e, totf, totb = b_const, e_const, totf_const, totb_const
        kf = (k * jnp.exp(totf - b)).astype(BF16)
        kb = (k * jnp.exp(e)).astype(BF16)
        kvf = lax.dot_general(v, kf, _TN, preferred_element_type=F32)
        kvb = lax.dot_general(v, kb, _TN, preferred_element_type=F32)
        kvf_scr[n] = jnp.where(same_head, kvf, 0.0)
        rnext_scr[n] = r_state.astype(BF16)
        return jnp.exp(totb) * r_state + jnp.where(same_head, kvb, 0.0)

    lax.fori_loop(0, n_chunks, phase1, jnp.zeros((vw, kw), F32), unroll=LIN_UNROLL)

    def phase2(n, s_state):
        rows = chunk_rows(n)
        v = v_ref[rows, :]
        if mode == "gla":
            q = q_ref[rows, :].astype(F32)
            k = k_ref[rows, :].astype(F32)
            b, e, totf, totb = gla_decays(rows)
        else:
            q = rotary(q_ref[rows, :], rows)
            k = kx_scr[rows, :]
            b, e, totf, totb = b_const, e_const, totf_const, totb_const

        qf = (q * jnp.exp(b)).astype(BF16)
        qb = (q * jnp.exp(totb - e)).astype(BF16)
        inter = (lax.dot_general(qf, s_state.astype(BF16), _NT, preferred_element_type=F32)
                 + lax.dot_general(qb, rnext_scr[n], _NT, preferred_element_type=F32))

        if mode == "gla":
            bm = b[c // 2 - 1:c // 2, :]
            em = e[c // 2:c // 2 + 1, :]
            qtf, ktf = q * jnp.exp(b - bm), (k * jnp.exp(bm - b)).astype(BF16)
            qtb, ktb = q * jnp.exp(em - e), (k * jnp.exp(e - em)).astype(BF16)
        else:
            kb16 = k.astype(BF16)
        outs = []
        for h in range(PAIR_HEADS):
            mh = head0_k if h == 0 else jnp.logical_not(head0_k)
            if mode == "gla":
                sf = lax.dot_general(jnp.where(mh, qtf, 0.0).astype(BF16), ktf, _NT, preferred_element_type=F32)
                sb = lax.dot_general(jnp.where(mh, qtb, 0.0).astype(BF16), ktb, _NT, preferred_element_type=F32)
                sc = jnp.where(lower, sf, 0.0) + jnp.where(upper, sb, 0.0)
            else:
                sc = lax.dot_general(jnp.where(mh, q, 0.0).astype(BF16), kb16, _NT,
                                     preferred_element_type=F32) * dmask[h]
            outs.append(_dot(sc.astype(BF16), v[:, h * HEAD_V:(h + 1) * HEAD_V]))

        gate = gate_ref[rows, :].astype(F32)
        for h in range(PAIR_HEADS):
            cols = slice(h * HEAD_V, (h + 1) * HEAD_V)
            o = outs[h] + inter[:, cols]
            o = _rms_rows(o, gain_ref[:, cols]) * _silu(gate[:, cols])
            o_ref[rows, cols] = o.astype(BF16)

        kvf = kvf_scr[n]
        return jnp.exp(totf) * s_state + kvf

    lax.fori_loop(0, n_chunks, phase2, jnp.zeros((vw, kw), F32), unroll=LIN_UNROLL)


def _linattn(mode, proj, col0, gain, extra, batch, t):
    chunk = GLA_CHUNK if mode == "gla" else RET_CHUNK
    n_chunks = t // chunk
    kw, vw = PAIR_HEADS * HEAD_K, PAIR_HEADS * HEAD_V
    v0 = (col0 + 2 * N_PAIRS) // 2
    g0 = v0 + N_PAIRS
    in_specs = [
        pl.BlockSpec((t, kw), lambda b, p: (b, col0 + p)),
        pl.BlockSpec((t, kw), lambda b, p: (b, col0 + N_PAIRS + p)),
        pl.BlockSpec((t, vw), lambda b, p: (b, v0 + p)),
        pl.BlockSpec((t, vw), lambda b, p: (b, g0 + p)),
        pl.BlockSpec((1, vw), lambda b, p: (0, p)),
    ]
    args = [proj, proj, proj, proj, gain]
    state_scratch = [pltpu.VMEM((n_chunks, vw, kw), F32), pltpu.VMEM((n_chunks, vw, kw), BF16)]
    if mode == "gla":
        ga, wf, wb, gb = extra
        in_specs += [
            pl.BlockSpec((t, LANES), lambda b, p: (b, 0)),
            pl.BlockSpec((LANES, kw), lambda b, p: (0, p)),
            pl.BlockSpec((LANES, kw), lambda b, p: (0, p)),
            pl.BlockSpec((2, kw), lambda b, p: (0, p)),
        ]
        args += [ga, wf, wb, gb]
        scratch = [pltpu.VMEM((t, kw), F32)] * 3 + state_scratch
    else:
        cos, sin, logit = extra
        in_specs += [
            pl.BlockSpec((t, kw), lambda b, p: (0, 0)),
            pl.BlockSpec((t, kw), lambda b, p: (0, 0)),
            pl.BlockSpec((2, kw), lambda b, p: (0, p)),
        ]
        args += [cos, sin, logit]
        scratch = [pltpu.VMEM((t, kw), F32)] + state_scratch
    return pl.pallas_call(
        functools.partial(_linattn_kernel, mode=mode, chunk=chunk, n_chunks=n_chunks),
        grid=(batch, N_PAIRS),
        in_specs=in_specs,
        out_specs=pl.BlockSpec((t, vw), lambda b, p: (b, p)),
        out_shape=jax.ShapeDtypeStruct((batch * t, N_PAIRS * vw), BF16),
        scratch_shapes=scratch,
        compiler_params=_params(("parallel", "arbitrary")),
        name="linattn_" + mode,
    )(*args)


def _outproj_kernel(x_ref, a_ref, b_ref, c_ref, wa_ref, wb_ref, wc_ref, o_ref):
    o_ref[...] = (x_ref[...] + _dot(a_ref[...], wa_ref[...]) + _dot(b_ref[...], wb_ref[...])
                  + _dot(c_ref[...], wc_ref[...]))


def _outproj(x2, a, b, c, w_o, layer):
    m, d = x2.shape
    tm, tn = min(MM_TILE_M // 2, m), d
    wa_rows, wb_rows = a.shape[1], b.shape[1]
    nb = wa_rows // wb_rows
    return pl.pallas_call(
        _outproj_kernel,
        grid=(m // tm, d // tn),
        in_specs=[
            pl.BlockSpec((tm, tn), lambda i, j: (i, j)),
            pl.BlockSpec((tm, wa_rows), lambda i, j: (i, 0)),
            pl.BlockSpec((tm, wb_rows), lambda i, j: (i, 0)),
            pl.BlockSpec((tm, wb_rows), lambda i, j: (i, 0)),
            pl.BlockSpec((None, wa_rows, tn), lambda i, j: (layer, 0, j)),
            pl.BlockSpec((None, wb_rows, tn), lambda i, j: (layer, nb, j)),
            pl.BlockSpec((None, wb_rows, tn), lambda i, j: (layer, nb + 1, j)),
        ],
        out_specs=pl.BlockSpec((tm, tn), lambda i, j: (i, j)),
        out_shape=jax.ShapeDtypeStruct((m, d), F32),
        compiler_params=_params(("parallel", "arbitrary")),
        name="outproj",
    )(x2, a, b, c, w_o, w_o, w_o)


def _ffn_up_kernel(x_ref, xp_ref, xn_ref, g_ref, wa_ref, wg_ref, cwa_ref, cwg_ref, cba_ref, cbg_ref,
                   o_ref, h_scr, *, tiles_per_seq):
    i = pl.program_id(0)
    tm = x_ref.shape[0]

    @pl.when(pl.program_id(1) == 0)
    def _():
        g = g_ref[...]
        keep_prev = (i % tiles_per_seq != 0).astype(F32)
        keep_next = ((i + 1) % tiles_per_seq != 0).astype(F32)
        h_scr[0:HALO, :] = (_rms_rows(xp_ref[...], g) * keep_prev).astype(BF16)
        h_scr[HALO:HALO + tm, :] = _rms_rows(x_ref[...], g).astype(BF16)
        h_scr[HALO + tm:, :] = (_rms_rows(xn_ref[...], g) * keep_next).astype(BF16)

    h = h_scr[...]
    n_ext = tm + 2 * HALO

    def conv(w_ref, cw_ref, cb_ref):
        u = _dot(h, w_ref[...].astype(BF16))
        prev = pltpu.roll(u, 1, 0)[HALO:HALO + tm]
        nxt = pltpu.roll(u, n_ext - 1, 0)[HALO:HALO + tm]
        return prev * cw_ref[0:1, :] + u[HALO:HALO + tm] * cw_ref[1:2, :] + nxt * cw_ref[2:3, :] + cb_ref[...]

    gate = _silu(conv(wg_ref, cwg_ref, cbg_ref))
    o_ref[...] = (conv(wa_ref, cwa_ref, cba_ref) * gate).astype(BF16)


def _ffn_up(x2, g, w_up, conv_w, conv_b, t, layer):
    m, d = x2.shape
    d_ff = w_up.shape[-1] // 2
    tm = min(MM_TILE_M, t)
    tn = FFN_TILE_N
    nj = d_ff // tn
    blocks_per_tile = tm // HALO
    last_block = m // HALO - 1
    return pl.pallas_call(
        functools.partial(_ffn_up_kernel, tiles_per_seq=t // tm),
        grid=(m // tm, nj),
        in_specs=[
            pl.BlockSpec((tm, d), lambda i, j: (i, 0)),
            pl.BlockSpec((HALO, d), lambda i, j: (jnp.maximum(i * blocks_per_tile - 1, 0), 0)),
            pl.BlockSpec((HALO, d), lambda i, j: (jnp.minimum((i + 1) * blocks_per_tile, last_block), 0)),
            pl.BlockSpec((1, d), lambda i, j: (0, 0)),
            pl.BlockSpec((None, d, tn), lambda i, j: (layer, 0, j)),
            pl.BlockSpec((None, d, tn), lambda i, j: (layer, 0, nj + j)),
            pl.BlockSpec((CONV_W, tn), lambda i, j: (0, j)),
            pl.BlockSpec((CONV_W, tn), lambda i, j: (0, nj + j)),
            pl.BlockSpec((1, tn), lambda i, j: (0, j)),
            pl.BlockSpec((1, tn), lambda i, j: (0, nj + j)),
        ],
        out_specs=pl.BlockSpec((tm, tn), lambda i, j: (i, j)),
        out_shape=jax.ShapeDtypeStruct((m, d_ff), BF16),
        scratch_shapes=[pltpu.VMEM((tm + 2 * HALO, d), BF16)],
        compiler_params=_params(("parallel", "arbitrary")),
        name="ffn_up",
    )(x2, x2, x2, g, w_up, w_up, conv_w, conv_w, conv_b, conv_b)


def _ffn_down_kernel(x_ref, a_ref, w_ref, o_ref):
    o_ref[...] = x_ref[...] + _dot(a_ref[...], w_ref[...].astype(BF16))


def _ffn_down(x2, act, w_down, layer):
    m, d = x2.shape
    k = act.shape[1]
    tm, tn = min(MM_TILE_M, m), FFN_DOWN_TILE_N
    return pl.pallas_call(
        _ffn_down_kernel,
        grid=(m // tm, d // tn),
        in_specs=[
            pl.BlockSpec((tm, tn), lambda i, j: (i, j)),
            pl.BlockSpec((tm, k), lambda i, j: (i, 0)),
            pl.BlockSpec((None, k, tn), lambda i, j: (layer, 0, j)),
        ],
        out_specs=pl.BlockSpec((tm, tn), lambda i, j: (i, j)),
        out_shape=jax.ShapeDtypeStruct((m, d), F32),
        compiler_params=_params(("parallel", "arbitrary")),
        name="ffn_down",
    )(x2, act, w_down)


def _rmsnorm_kernel(x_ref, g_ref, o_ref):
    o_ref[...] = _rms_rows(x_ref[...], g_ref[...])


def _rmsnorm(x2, g):
    m, d = x2.shape
    tm = min(MM_TILE_M, m)
    return pl.pallas_call(
        _rmsnorm_kernel,
        grid=(m // tm,),
        in_specs=[pl.BlockSpec((tm, d), lambda i: (i, 0)), pl.BlockSpec((1, d), lambda i: (0, 0))],
        out_specs=pl.BlockSpec((tm, d), lambda i: (i, 0)),
        out_shape=jax.ShapeDtypeStruct((m, d), F32),
        compiler_params=_params(("parallel",)),
        name="final_rmsnorm",
    )(x2, g)


def _rotary_tables(t):
    half = HEAD_K // 2
    inv = 1.0 / (ROPE_BASE ** jnp.linspace(0.0, 1.0, half, dtype=F32))
    ang = jnp.arange(t, dtype=jnp.int32).astype(F32)[:, None] * inv[None, :]
    cos, sin = jnp.cos(ang), jnp.sin(ang)
    cos_l = jnp.tile(cos, (1, 2 * PAIR_HEADS))
    sin_l = jnp.tile(jnp.concatenate([-sin, sin], axis=-1), (1, PAIR_HEADS))
    return cos_l, sin_l


def kernel(x, ln1_g, w_in, diff_lambda, rel_bias, gla_gate_w, gla_gate_b, ret_decay_logit, head_gain,
           w_o, ln2_g, w_up, conv_w, conv_b, w_down, final_g):
    batch, t, d = x.shape
    depth = w_in.shape[0]
    diff_w = DIFF_HEADS * HEAD_V
    gla_w = 2 * N_PAIRS * HEAD_V
    gla_k = 2 * N_PAIRS * HEAD_K
    ga0 = 3 * diff_w + 2 * gla_k + 2 * gla_w
    ga1 = ga0 + 2 * GLA_GATE_RANK
    gla_col0 = 3 * diff_w // LANES
    ret_col0 = ga0 // LANES

    w_main = jnp.concatenate([w_in[:, :, :ga0], w_in[:, :, ga1:]], axis=-1).astype(BF16)
    w_ga = jnp.pad(w_in[:, :, ga0:ga1], ((0, 0), (0, 0), (0, LANES - 2 * GLA_GATE_RANK))).astype(BF16)
    w_o16 = w_o.astype(BF16)
    k_scale = HEAD_K ** -0.5
    colscale = jnp.ones((1, w_main.shape[-1]), F32)
    colscale = colscale.at[:, :diff_w].set(k_scale * LOG2E)
    colscale = colscale.at[:, 3 * diff_w:3 * diff_w + gla_k].set(k_scale)
    colscale = colscale.at[:, ga0 + gla_k:ga0 + 2 * gla_k].set(k_scale)
    r = GLA_GATE_RANK
    gate_wf = jnp.pad(gla_gate_w[:, 0], ((0, 0), (0, LANES - r), (0, 0)))
    gate_wb = jnp.pad(gla_gate_w[:, 1], ((0, 0), (r, LANES - 2 * r), (0, 0)))
    ret_logit = jnp.repeat(ret_decay_logit, HEAD_K, axis=-1)

    rel = (jnp.arange(ATT_TILE, dtype=jnp.int32)[:, None] - jnp.arange(ATT_TILE, dtype=jnp.int32)[None, :])
    rel = rel[None] + ATT_TILE * (jnp.arange(5, dtype=jnp.int32) - 2)[:, None, None]
    bias_tiles = _bias_tiles(rel_bias.astype(F32), _t5_bucket(rel))
    cos_l, sin_l = _rotary_tables(t)

    x2 = x.reshape(batch * t, d)
    for layer in range(depth):
        lam_init = 0.8 - 0.6 * math.exp(-0.3 * layer)
        gain = head_gain[layer][None, :]
        proj, ga = _inproj(x2, ln1_g[layer][None, :], w_main, w_ga, colscale, layer)
        a_out = _diff_attention(proj, bias_tiles, diff_lambda[layer], gain[:, :diff_w], lam_init, batch, t)
        b_out = _linattn("gla", proj, gla_col0, gain[:, diff_w:diff_w + gla_w],
                         (ga, gate_wf[layer], gate_wb[layer], gla_gate_b[layer]), batch, t)
        c_out = _linattn("ret", proj, ret_col0, gain[:, diff_w + gla_w:],
                         (cos_l, sin_l, ret_logit[layer]), batch, t)
        x2 = _outproj(x2, a_out, b_out, c_out, w_o16, layer)
        act = _ffn_up(x2, ln2_g[layer][None, :], w_up, conv_w[layer], conv_b[layer][None, :], t, layer)
        x2 = _ffn_down(x2, act, w_down, layer)
    return _rmsnorm(x2, final_g[None, :]).reshape(batch, t, d)
```

```python
import functools
import math

import jax
import jax.numpy as jnp
from jax import lax
from jax.experimental import pallas as pl
from jax.experimental.pallas import tpu as pltpu

F32 = jnp.float32
BF16 = jnp.bfloat16

HEAD_V = 128
DIFF_HEADS = 8
PAIR_HEADS = 2
N_PAIRS = 2
HEAD_K = 64
GLA_GATE_RANK = 16
GLA_TAU = 16.0
ROPE_BASE = 10000.0
N_BUCKETS = 32
MAX_DISTANCE = 128
CONV_W = 3
EPS = 1e-6
LOG2E = math.log2(math.e)

LANES = 128
SUBLANES = 8
BF16_ROWS = 16
VMEM_LIMIT_BYTES = 56 * 1024 * 1024

ATT_TILE = 256
ATT_QTILE = 1024
ATT_KSTAGE = 1024
GLA_CHUNK = 64
RET_CHUNK = 128
LIN_UNROLL = 8
MM_TILE_M = 1024
MM_TILE_N = 1024
FFN_TILE_N = 512
HALO = SUBLANES

NEG_BIG = -1e30

_NT = (((1,), (1,)), ((), ()))
_TN = (((0,), (0,)), ((), ()))


def _params(sem):
    return pltpu.CompilerParams(dimension_semantics=sem, vmem_limit_bytes=VMEM_LIMIT_BYTES)


def _dot(a, b):
    return jnp.dot(a, b, preferred_element_type=F32)


def _rms_rows(x, g):
    return x * lax.rsqrt(jnp.mean(x * x, axis=-1, keepdims=True) + EPS) * g


def _log_sigmoid(z):
    return jnp.minimum(z, 0.0) - jnp.log(1.0 + jnp.exp(-jnp.abs(z)))


def _silu(z):
    return z * (1.0 / (1.0 + jnp.exp(-z)))


def _inproj_kernel(x_ref, g_ref, w_ref, wga_ref, cs_ref, proj_ref, ga_ref, h_scr):
    @pl.when(pl.program_id(1) == 0)
    def _():
        hb = _rms_rows(x_ref[...], g_ref[...]).astype(BF16)
        h_scr[...] = hb
        ga_ref[...] = _dot(hb, wga_ref[...])

    proj_ref[...] = (_dot(h_scr[...], w_ref[...]) * cs_ref[...]).astype(BF16)


def _inproj(x2, g, w, wga, colscale, layer):
    m, d = x2.shape
    n = w.shape[-1]
    tm, tn = min(MM_TILE_M, m), min(MM_TILE_N, n)
    return pl.pallas_call(
        _inproj_kernel,
        grid=(m // tm, n // tn),
        in_specs=[
            pl.BlockSpec((tm, d), lambda i, j: (i, 0)),
            pl.BlockSpec((1, d), lambda i, j: (0, 0)),
            pl.BlockSpec((None, d, tn), lambda i, j: (layer, 0, j)),
            pl.BlockSpec((None, d, LANES), lambda i, j: (layer, 0, 0)),
            pl.BlockSpec((1, tn), lambda i, j: (0, j)),
        ],
        out_specs=[
            pl.BlockSpec((tm, tn), lambda i, j: (i, j)),
            pl.BlockSpec((tm, LANES), lambda i, j: (i, 0)),
        ],
        out_shape=[jax.ShapeDtypeStruct((m, n), BF16), jax.ShapeDtypeStruct((m, LANES), F32)],
        scratch_shapes=[pltpu.VMEM((tm, d), BF16)],
        compiler_params=_params(("parallel", "arbitrary")),
        name="inproj",
    )(x2, g, w, wga, colscale)


def _bias_tiles_kernel(rb_ref, bucket_ref, out_ref):
    h = pl.program_id(0)
    bucket = bucket_ref[...]
    acc = jnp.zeros(bucket.shape, F32)
    for b in range(N_BUCKETS):
        acc = jnp.where(bucket == b, rb_ref[b, h], acc)
    out_ref[...] = acc * LOG2E


def _bias_tiles(rel_bias, bucket):
    nd = bucket.shape[0]
    return pl.pallas_call(
        _bias_tiles_kernel,
        grid=(DIFF_HEADS,),
        in_specs=[
            pl.BlockSpec(memory_space=pltpu.SMEM),
            pl.BlockSpec((nd, ATT_TILE, ATT_TILE), lambda h: (0, 0, 0)),
        ],
        out_specs=pl.BlockSpec((None, nd, ATT_TILE, ATT_TILE), lambda h: (h, 0, 0, 0)),
        out_shape=jax.ShapeDtypeStruct((DIFF_HEADS, nd, ATT_TILE, ATT_TILE), F32),
        compiler_params=_params(("arbitrary",)),
        name="bias_tiles",
    )(rel_bias, bucket)


def _t5_bucket(rel):
    half = N_BUCKETS // 2
    max_exact = half // 2
    ret = jnp.where(rel > 0, half, 0)
    n = jnp.abs(rel)
    nf = jnp.maximum(n, 1).astype(F32)
    large = max_exact + (jnp.log(nf / max_exact) / math.log(MAX_DISTANCE / max_exact)
                         * (half - max_exact)).astype(jnp.int32)
    large = jnp.minimum(large, half - 1)
    return ret + jnp.where(n < max_exact, n, large)


ACC_ROWS = HEAD_V + BF16_ROWS


def _attn_kernel(q_ref, k_ref, v_ref, bias_ref, lam_ref, gain_ref, o_ref,
                 vt_scr, qh_scr, s_scr, p_scr, mloc_scr, alpha_scr, m_scr, acc_scr,
                 *, lam_init, n_stages, stage_keys, q_tile_rows):
    tile = ATT_TILE
    tq, ks = q_tile_rows, stage_keys
    n_qb = tq // tile
    n_kb = ks // tile
    ones_rows = (lax.broadcasted_iota(jnp.int32, (BF16_ROWS, ks), 0) == 0).astype(BF16)
    for c in range(n_stages):
        for kb in range(n_kb):
            vc = v_ref[c * ks + kb * tile:c * ks + (kb + 1) * tile, :].astype(F32)
            vt_scr[c, 0:HEAD_V, kb * tile:(kb + 1) * tile] = vc.T.astype(BF16)
        vt_scr[c, HEAD_V:ACC_ROWS, :] = ones_rows

    lp = lam_ref[...]
    lam = (jnp.exp(jnp.sum(lp[0:1] * lp[1:2], axis=-1, keepdims=True))
           - jnp.exp(jnp.sum(lp[2:3] * lp[3:4], axis=-1, keepdims=True)) + lam_init)
    first_half = lax.broadcasted_iota(jnp.int32, (1, 2 * HEAD_K), 1) < HEAD_K
    gain = gain_ref[...] * (1.0 - lam_init)

    def q_tile(qi, _):
        rows = pl.ds(pl.multiple_of(qi * tq, tq), tq)
        q = q_ref[rows, :]
        zero = jnp.zeros_like(q)
        qh_scr[0] = jnp.where(first_half, q, zero)
        qh_scr[1] = jnp.where(first_half, zero, q)
        for half in range(2):
            m_scr[half] = jnp.full((1, tq), NEG_BIG, F32)
            acc_scr[half, :, 0:tq] = jnp.zeros((ACC_ROWS, tq), F32)

        def stage_a(c):
            kc = k_ref[pl.ds(pl.multiple_of(c * ks, ks), ks), :]
            for qb in range(n_qb):
                cols = slice(qb * tile, (qb + 1) * tile)
                s = [lax.dot_general(kc, qh_scr[half, cols, :], _NT, preferred_element_type=F32)
                     for half in range(2)]
                mloc = [None, None]
                for kb in range(n_kb):
                    krows = slice(kb * tile, (kb + 1) * tile)
                    bias = bias_ref[jnp.clip((c * n_kb + kb) - (qi * n_qb + qb), -2, 2) + 2]
                    for half in range(2):
                        sb = s[half][krows] + bias
                        s_scr[half, krows, cols] = sb
                        bmax = jnp.max(sb, axis=0, keepdims=True)
                        mloc[half] = bmax if kb == 0 else jnp.maximum(mloc[half], bmax)
                for half in range(2):
                    mloc_scr[half, :, cols] = mloc[half]

        def stage_b():
            for half in range(2):
                m_old = m_scr[half]
                m_new = jnp.maximum(m_old, mloc_scr[half])
                alpha_scr[half] = jnp.exp2(m_old - m_new)
                m_scr[half] = m_new
                p_scr[half, :, 0:tq] = jnp.exp2(s_scr[half, :, 0:tq] - m_new).astype(BF16)

        def stage_c(c):
            vt = vt_scr[c]
            for half in range(2):
                acc_scr[half, :, 0:tq] = (acc_scr[half, :, 0:tq] * alpha_scr[half]
                                          + _dot(vt, p_scr[half, :, 0:tq]))

        stage_a(0)
        stage_b()
        stage_a(1)

        def body(c, _):
            stage_c(c)
            stage_b()
            stage_a(c + 2)
            return 0

        lax.fori_loop(0, n_stages - 2, body, 0)
        stage_c(n_stages - 2)
        stage_b()
        stage_c(n_stages - 1)

        acc1, acc2 = acc_scr[0, :, 0:tq], acc_scr[1, :, 0:tq]
        o1 = acc1[0:HEAD_V] / acc1[HEAD_V:HEAD_V + 1]
        o2 = acc2[0:HEAD_V] / acc2[HEAD_V:HEAD_V + 1]
        ot = o1 - lam * o2
        ot = ot * lax.rsqrt(jnp.mean(ot * ot, axis=0, keepdims=True) + EPS)
        o_ref[rows, :] = (ot.T * gain).astype(BF16)
        return 0

    lax.fori_loop(0, (n_stages * ks) // tq, q_tile, 0)


def _diff_attention(proj, bias_tiles, lam_params, gain, lam_init, batch, t):
    nd = bias_tiles.shape[1]
    tile = ATT_TILE
    tq = min(ATT_QTILE, t)
    ks = min(ATT_KSTAGE, t // 2)
    n_stages = t // ks
    return pl.pallas_call(
        functools.partial(_attn_kernel, lam_init=lam_init, n_stages=n_stages, stage_keys=ks, q_tile_rows=tq),
        grid=(batch, DIFF_HEADS),
        in_specs=[
            pl.BlockSpec((t, HEAD_V), lambda b, h: (b, h)),
            pl.BlockSpec((t, HEAD_V), lambda b, h: (b, DIFF_HEADS + h)),
            pl.BlockSpec((t, HEAD_V), lambda b, h: (b, 2 * DIFF_HEADS + h)),
            pl.BlockSpec((None, nd, tile, tile), lambda b, h: (h, 0, 0, 0)),
            pl.BlockSpec(lam_params.shape, lambda b, h: (0, 0)),
            pl.BlockSpec((1, HEAD_V), lambda b, h: (0, h)),
        ],
        out_specs=pl.BlockSpec((t, HEAD_V), lambda b, h: (b, h)),
        out_shape=jax.ShapeDtypeStruct((batch * t, DIFF_HEADS * HEAD_V), BF16),
        scratch_shapes=[
            pltpu.VMEM((n_stages, ACC_ROWS, ks), BF16),
            pltpu.VMEM((2, tq, 2 * HEAD_K), BF16),
            pltpu.VMEM((2, ks, tq + LANES), F32),
            pltpu.VMEM((2, ks, tq + LANES), BF16),
            pltpu.VMEM((2, 1, tq), F32),
            pltpu.VMEM((2, 1, tq), F32),
            pltpu.VMEM((2, 1, tq), F32),
            pltpu.VMEM((2, ACC_ROWS, tq + LANES), F32),
        ],
        compiler_params=_params(("parallel", "arbitrary")),
        name="diff_attn",
    )(proj, proj, proj, bias_tiles, lam_params, gain)


def _linattn_kernel(*refs, mode, chunk, n_chunks):
    if mode == "gla":
        (q_ref, k_ref, v_ref, gate_ref, gain_ref, ga_ref, wf_ref, wb_ref, gb_ref,
         o_ref, b_scr, einc_scr, lab_scr, kvf_scr, rnext_scr) = refs
    else:
        (q_ref, k_ref, v_ref, gate_ref, gain_ref, cos_ref, sin_ref, logit_ref,
         o_ref, kx_scr, kvf_scr, rnext_scr) = refs
    c = chunk
    kw = PAIR_HEADS * HEAD_K
    vw = PAIR_HEADS * HEAD_V

    lane_k = lax.broadcasted_iota(jnp.int32, (1, kw), 1)
    head0_k = lane_k < HEAD_K
    st_row = lax.broadcasted_iota(jnp.int32, (vw, kw), 0)
    st_lane = lax.broadcasted_iota(jnp.int32, (vw, kw), 1)
    same_head = (st_row < HEAD_V) == (st_lane < HEAD_K)
    ri = lax.broadcasted_iota(jnp.int32, (c, c), 0)
    ci = lax.broadcasted_iota(jnp.int32, (c, c), 1)
    lower, upper = ri >= ci, ri <= ci

    def chunk_rows(n):
        return pl.ds(pl.multiple_of(n * c, c), c)

    if mode == "ret":
        pr = lax.broadcasted_iota(jnp.int32, (kw, kw), 0)
        pc = lax.broadcasted_iota(jnp.int32, (kw, kw), 1)
        swap = (pr == (pc ^ (HEAD_K // 2))).astype(BF16)

        def rotary(x_bf16, rows):
            xr = _dot(x_bf16, swap)
            return x_bf16.astype(F32) * cos_ref[rows, :] + xr * sin_ref[rows, :]

        lg = _log_sigmoid(logit_ref[...])
        lgf, lgb = lg[0:1], lg[1:2]
        pos = lax.broadcasted_iota(jnp.int32, (c, kw), 0).astype(F32)
        b_const, e_const = (pos + 1.0) * lgf, pos * lgb
        totf_const, totb_const = float(c) * lgf, float(c) * lgb
        rel = (ri - ci).astype(F32)
        dmask = []
        for h in range(PAIR_HEADS):
            lf = lgf[:, h * HEAD_K:h * HEAD_K + 1]
            lb = lgb[:, h * HEAD_K:h * HEAD_K + 1]
            dmask.append(jnp.where(lower, jnp.exp(lf * jnp.maximum(rel, 0.0)), 0.0)
                         + jnp.where(upper, jnp.exp(lb * jnp.maximum(-rel, 0.0)), 0.0))
    else:
        ga = ga_ref[...].astype(BF16)
        b_scr[...] = _log_sigmoid(_dot(ga, wf_ref[...]) + gb_ref[0:1, :]) * (1.0 / GLA_TAU)
        lab_scr[...] = _log_sigmoid(_dot(ga, wb_ref[...]) + gb_ref[1:2, :]) * (1.0 / GLA_TAU)
        row_in_chunk = lax.broadcasted_iota(jnp.int32, (c, kw), 0)

        def chunk_cumsum(x):
            shift = 1
            while shift < c:
                x = x + jnp.where(row_in_chunk >= shift, pltpu.roll(x, shift, 0), 0.0)
                shift *= 2
            return x

    def phase1(i, r_state):
        n = n_chunks - 1 - i
        rows = chunk_rows(n)
        v = v_ref[rows, :]
        if mode == "gla":
            k = k_ref[rows, :].astype(F32)
            lab = lab_scr[rows, :]
            b, einc = chunk_cumsum(b_scr[rows, :]), chunk_cumsum(lab)
            b_scr[rows, :] = b
            einc_scr[rows, :] = einc
            e, totf, totb = einc - lab, b[c - 1:c, :], einc[c - 1:c, :]
        else:
            k = rotary(k_ref[rows, :], rows)
            kx_scr[rows, :] = k
            b, e, totf, totb = b_const, e_const, totf_const, totb_const
        kf = (k * jnp.exp(totf - b)).astype(BF16)
        kb = (k * jnp.exp(e)).astype(BF16)
        kvf = lax.dot_general(v, kf, _TN, preferred_element_type=F32)
        kvb = lax.dot_general(v, kb, _TN, preferred_element_type=F32)
        kvf_scr[n] = jnp.where(same_head, kvf, 0.0)
        rnext_scr[n] = r_state.astype(BF16)
        return jnp.exp(totb) * r_state + jnp.where(same_head, kvb, 0.0)

    lax.fori_loop(0, n_chunks, phase1, jnp.zeros((vw, kw), F32), unroll=LIN_UNROLL)

    def phase2(n, s_state):
        rows = chunk_rows(n)
        v = v_ref[rows, :]
        if mode == "gla":
            q = q_ref[rows, :].astype(F32)
            k = k_ref[rows, :].astype(F32)
            b, einc = b_scr[rows, :], einc_scr[rows, :]
            e, totf, totb = einc - lab_scr[rows, :], b[c - 1:c, :], einc[c - 1:c, :]
        else:
            q = rotary(q_ref[rows, :], rows)
            k = kx_scr[rows, :]
            b, e, totf, totb = b_const, e_const, totf_const, totb_const

        qf = (q * jnp.exp(b)).astype(BF16)
        qb = (q * jnp.exp(totb - e)).astype(BF16)
        inter = (lax.dot_general(qf, s_state.astype(BF16), _NT, preferred_element_type=F32)
                 + lax.dot_general(qb, rnext_scr[n], _NT, preferred_element_type=F32))

        if mode == "gla":
            bm = b[c // 2 - 1:c // 2, :]
            em = e[c // 2:c // 2 + 1, :]
            qtf, ktf = q * jnp.exp(b - bm), (k * jnp.exp(bm - b)).astype(BF16)
            qtb, ktb = q * jnp.exp(em - e), (k * jnp.exp(e - em)).astype(BF16)
        else:
            kb16 = k.astype(BF16)
        outs = []
        for h in range(PAIR_HEADS):
            mh = head0_k if h == 0 else jnp.logical_not(head0_k)
            if mode == "gla":
                sf = lax.dot_general(jnp.where(mh, qtf, 0.0).astype(BF16), ktf, _NT, preferred_element_type=F32)
                sb = lax.dot_general(jnp.where(mh, qtb, 0.0).astype(BF16), ktb, _NT, preferred_element_type=F32)
                sc = jnp.where(lower, sf, 0.0) + jnp.where(upper, sb, 0.0)
            else:
                sc = lax.dot_general(jnp.where(mh, q, 0.0).astype(BF16), kb16, _NT,
                                     preferred_element_type=F32) * dmask[h]
            outs.append(_dot(sc.astype(BF16), v[:, h * HEAD_V:(h + 1) * HEAD_V]))

        gate = gate_ref[rows, :].astype(F32)
        for h in range(PAIR_HEADS):
            cols = slice(h * HEAD_V, (h + 1) * HEAD_V)
            o = outs[h] + inter[:, cols]
            o = _rms_rows(o, gain_ref[:, cols]) * _silu(gate[:, cols])
            o_ref[rows, cols] = o.astype(BF16)

        kvf = kvf_scr[n]
        return jnp.exp(totf) * s_state + kvf

    lax.fori_loop(0, n_chunks, phase2, jnp.zeros((vw, kw), F32), unroll=LIN_UNROLL)


def _linattn(mode, proj, col0, gain, extra, batch, t):
    chunk = GLA_CHUNK if mode == "gla" else RET_CHUNK
    n_chunks = t // chunk
    kw, vw = PAIR_HEADS * HEAD_K, PAIR_HEADS * HEAD_V
    v0 = (col0 + 2 * N_PAIRS) // 2
    g0 = v0 + N_PAIRS
    in_specs = [
        pl.BlockSpec((t, kw), lambda b, p: (b, col0 + p)),
        pl.BlockSpec((t, kw), lambda b, p: (b, col0 + N_PAIRS + p)),
        pl.BlockSpec((t, vw), lambda b, p: (b, v0 + p)),
        pl.BlockSpec((t, vw), lambda b, p: (b, g0 + p)),
        pl.BlockSpec((1, vw), lambda b, p: (0, p)),
    ]
    args = [proj, proj, proj, proj, gain]
    state_scratch = [pltpu.VMEM((n_chunks, vw, kw), F32), pltpu.VMEM((n_chunks, vw, kw), BF16)]
    if mode == "gla":
        ga, wf, wb, gb = extra
        in_specs += [
            pl.BlockSpec((t, LANES), lambda b, p: (b, 0)),
            pl.BlockSpec((LANES, kw), lambda b, p: (0, p)),
            pl.BlockSpec((LANES, kw), lambda b, p: (0, p)),
            pl.BlockSpec((2, kw), lambda b, p: (0, p)),
        ]
        args += [ga, wf, wb, gb]
        scratch = [pltpu.VMEM((t, kw), F32)] * 3 + state_scratch
    else:
        cos, sin, logit = extra
        in_specs += [
            pl.BlockSpec((t, kw), lambda b, p: (0, 0)),
            pl.BlockSpec((t, kw), lambda b, p: (0, 0)),
            pl.BlockSpec((2, kw), lambda b, p: (0, p)),
        ]
        args += [cos, sin, logit]
        scratch = [pltpu.VMEM((t, kw), F32)] + state_scratch
    return pl.pallas_call(
        functools.partial(_linattn_kernel, mode=mode, chunk=chunk, n_chunks=n_chunks),
        grid=(batch, N_PAIRS),
        in_specs=in_specs,
        out_specs=pl.BlockSpec((t, vw), lambda b, p: (b, p)),
        out_shape=jax.ShapeDtypeStruct((batch * t, N_PAIRS * vw), BF16),
        scratch_shapes=scratch,
        compiler_params=_params(("parallel", "arbitrary")),
        name="linattn_" + mode,
    )(*args)


def _outproj_kernel(x_ref, a_ref, b_ref, c_ref, wa_ref, wb_ref, wc_ref, o_ref):
    o_ref[...] = (x_ref[...] + _dot(a_ref[...], wa_ref[...].astype(BF16))
                  + _dot(b_ref[...], wb_ref[...].astype(BF16)) + _dot(c_ref[...], wc_ref[...].astype(BF16)))


def _outproj(x2, a, b, c, w_o, layer):
    m, d = x2.shape
    tm, tn = min(MM_TILE_M // 2, m), d
    wa_rows, wb_rows = a.shape[1], b.shape[1]
    nb = wa_rows // wb_rows
    return pl.pallas_call(
        _outproj_kernel,
        grid=(m // tm, d // tn),
        in_specs=[
            pl.BlockSpec((tm, tn), lambda i, j: (i, j)),
            pl.BlockSpec((tm, wa_rows), lambda i, j: (i, 0)),
            pl.BlockSpec((tm, wb_rows), lambda i, j: (i, 0)),
            pl.BlockSpec((tm, wb_rows), lambda i, j: (i, 0)),
            pl.BlockSpec((None, wa_rows, tn), lambda i, j: (layer, 0, j), pipeline_mode=pl.Buffered(1)),
            pl.BlockSpec((None, wb_rows, tn), lambda i, j: (layer, nb, j), pipeline_mode=pl.Buffered(1)),
            pl.BlockSpec((None, wb_rows, tn), lambda i, j: (layer, nb + 1, j), pipeline_mode=pl.Buffered(1)),
        ],
        out_specs=pl.BlockSpec((tm, tn), lambda i, j: (i, j)),
        out_shape=jax.ShapeDtypeStruct((m, d), F32),
        compiler_params=_params(("parallel", "arbitrary")),
        name="outproj",
    )(x2, a, b, c, w_o, w_o, w_o)


def _ffn_up_kernel(x_ref, xp_ref, xn_ref, g_ref, wa_ref, wg_ref, cwa_ref, cwg_ref, cba_ref, cbg_ref,
                   o_ref, h_scr, *, tiles_per_seq):
    i = pl.program_id(0)
    tm = x_ref.shape[0]

    @pl.when(pl.program_id(1) == 0)
    def _():
        g = g_ref[...]
        keep_prev = (i % tiles_per_seq != 0).astype(F32)
        keep_next = ((i + 1) % tiles_per_seq != 0).astype(F32)
        h_scr[0:HALO, :] = (_rms_rows(xp_ref[...], g) * keep_prev).astype(BF16)
        h_scr[HALO:HALO + tm, :] = _rms_rows(x_ref[...], g).astype(BF16)
        h_scr[HALO + tm:, :] = (_rms_rows(xn_ref[...], g) * keep_next).astype(BF16)

    h = h_scr[...]
    n_ext = tm + 2 * HALO

    def conv(w_ref, cw_ref, cb_ref):
        u = _dot(h, w_ref[...].astype(BF16))
        prev = pltpu.roll(u, 1, 0)[HALO:HALO + tm]
        nxt = pltpu.roll(u, n_ext - 1, 0)[HALO:HALO + tm]
        return prev * cw_ref[0:1, :] + u[HALO:HALO + tm] * cw_ref[1:2, :] + nxt * cw_ref[2:3, :] + cb_ref[...]

    gate = _silu(conv(wg_ref, cwg_ref, cbg_ref))
    o_ref[...] = (conv(wa_ref, cwa_ref, cba_ref) * gate).astype(BF16)


def _ffn_up(x2, g, w_up, conv_w, conv_b, t, layer):
    m, d = x2.shape
    d_ff = w_up.shape[-1] // 2
    tm = min(MM_TILE_M, t)
    tn = FFN_TILE_N
    nj = d_ff // tn
    blocks_per_tile = tm // HALO
    last_block = m // HALO - 1
    return pl.pallas_call(
        functools.partial(_ffn_up_kernel, tiles_per_seq=t // tm),
        grid=(m // tm, nj),
        in_specs=[
            pl.BlockSpec((tm, d), lambda i, j: (i, 0)),
            pl.BlockSpec((HALO, d), lambda i, j: (jnp.maximum(i * blocks_per_tile - 1, 0), 0)),
            pl.BlockSpec((HALO, d), lambda i, j: (jnp.minimum((i + 1) * blocks_per_tile, last_block), 0)),
            pl.BlockSpec((1, d), lambda i, j: (0, 0)),
            pl.BlockSpec((None, d, tn), lambda i, j: (layer, 0, j)),
            pl.BlockSpec((None, d, tn), lambda i, j: (layer, 0, nj + j)),
            pl.BlockSpec((CONV_W, tn), lambda i, j: (0, j)),
            pl.BlockSpec((CONV_W, tn), lambda i, j: (0, nj + j)),
            pl.BlockSpec((1, tn), lambda i, j: (0, j)),
            pl.BlockSpec((1, tn), lambda i, j: (0, nj + j)),
        ],
        out_specs=pl.BlockSpec((tm, tn), lambda i, j: (i, j)),
        out_shape=jax.ShapeDtypeStruct((m, d_ff), BF16),
        scratch_shapes=[pltpu.VMEM((tm + 2 * HALO, d), BF16)],
        compiler_params=_params(("parallel", "arbitrary")),
        name="ffn_up",
    )(x2, x2, x2, g, w_up, w_up, conv_w, conv_w, conv_b, conv_b)


def _ffn_down_kernel(x_ref, a_ref, w_ref, o_ref):
    o_ref[...] = x_ref[...] + _dot(a_ref[...], w_ref[...])


def _ffn_down(x2, act, w_down, layer):
    m, d = x2.shape
    k = act.shape[1]
    tm, tn = min(MM_TILE_M, m), FFN_TILE_N
    return pl.pallas_call(
        _ffn_down_kernel,
        grid=(m // tm, d // tn),
        in_specs=[
            pl.BlockSpec((tm, tn), lambda i, j: (i, j)),
            pl.BlockSpec((tm, k), lambda i, j: (i, 0)),
            pl.BlockSpec((None, k, tn), lambda i, j: (layer, 0, j)),
        ],
        out_specs=pl.BlockSpec((tm, tn), lambda i, j: (i, j)),
        out_shape=jax.ShapeDtypeStruct((m, d), F32),
        compiler_params=_params(("parallel", "arbitrary")),
        name="ffn_down",
    )(x2, act, w_down)


def _rmsnorm_kernel(x_ref, g_ref, o_ref):
    o_ref[...] = _rms_rows(x_ref[...], g_ref[...])


def _rmsnorm(x2, g):
    m, d = x2.shape
    tm = min(MM_TILE_M, m)
    return pl.pallas_call(
        _rmsnorm_kernel,
        grid=(m // tm,),
        in_specs=[pl.BlockSpec((tm, d), lambda i: (i, 0)), pl.BlockSpec((1, d), lambda i: (0, 0))],
        out_specs=pl.BlockSpec((tm, d), lambda i: (i, 0)),
        out_shape=jax.ShapeDtypeStruct((m, d), F32),
        compiler_params=_params(("parallel",)),
        name="final_rmsnorm",
    )(x2, g)


def _rotary_tables(t):
    half = HEAD_K // 2
    inv = 1.0 / (ROPE_BASE ** jnp.linspace(0.0, 1.0, half, dtype=F32))
    ang = jnp.arange(t, dtype=jnp.int32).astype(F32)[:, None] * inv[None, :]
    cos, sin = jnp.cos(ang), jnp.sin(ang)
    cos_l = jnp.tile(cos, (1, 2 * PAIR_HEADS))
    sin_l = jnp.tile(jnp.concatenate([-sin, sin], axis=-1), (1, PAIR_HEADS))
    return cos_l, sin_l


def kernel(x, ln1_g, w_in, diff_lambda, rel_bias, gla_gate_w, gla_gate_b, ret_decay_logit, head_gain,
           w_o, ln2_g, w_up, conv_w, conv_b, w_down, final_g):
    batch, t, d = x.shape
    depth = w_in.shape[0]
    diff_w = DIFF_HEADS * HEAD_V
    gla_w = 2 * N_PAIRS * HEAD_V
    gla_k = 2 * N_PAIRS * HEAD_K
    ga0 = 3 * diff_w + 2 * gla_k + 2 * gla_w
    ga1 = ga0 + 2 * GLA_GATE_RANK
    gla_col0 = 3 * diff_w // LANES
    ret_col0 = ga0 // LANES

    w_main = jnp.concatenate([w_in[:, :, :ga0], w_in[:, :, ga1:]], axis=-1).astype(BF16)
    w_ga = jnp.pad(w_in[:, :, ga0:ga1], ((0, 0), (0, 0), (0, LANES - 2 * GLA_GATE_RANK))).astype(BF16)
    w_down16 = w_down.astype(BF16)
    k_scale = HEAD_K ** -0.5
    colscale = jnp.ones((1, w_main.shape[-1]), F32)
    colscale = colscale.at[:, :diff_w].set(k_scale * LOG2E)
    colscale = colscale.at[:, 3 * diff_w:3 * diff_w + gla_k].set(k_scale)
    colscale = colscale.at[:, ga0 + gla_k:ga0 + 2 * gla_k].set(k_scale)
    r = GLA_GATE_RANK
    gate_wf = jnp.pad(gla_gate_w[:, 0], ((0, 0), (0, LANES - r), (0, 0))).astype(BF16)
    gate_wb = jnp.pad(gla_gate_w[:, 1], ((0, 0), (r, LANES - 2 * r), (0, 0))).astype(BF16)
    ret_logit = jnp.repeat(ret_decay_logit, HEAD_K, axis=-1)

    rel = (jnp.arange(ATT_TILE, dtype=jnp.int32)[:, None] - jnp.arange(ATT_TILE, dtype=jnp.int32)[None, :])
    rel = rel[None] + ATT_TILE * (jnp.arange(5, dtype=jnp.int32) - 2)[:, None, None]
    bias_tiles = _bias_tiles(rel_bias.astype(F32), _t5_bucket(rel))
    cos_l, sin_l = _rotary_tables(t)

    x2 = x.reshape(batch * t, d)
    for layer in range(depth):
        lam_init = 0.8 - 0.6 * math.exp(-0.3 * layer)
        gain = head_gain[layer][None, :]
        proj, ga = _inproj(x2, ln1_g[layer][None, :], w_main, w_ga, colscale, layer)
        a_out = _diff_attention(proj, bias_tiles, diff_lambda[layer], gain[:, :diff_w], lam_init, batch, t)
        b_out = _linattn("gla", proj, gla_col0, gain[:, diff_w:diff_w + gla_w],
                         (ga, gate_wf[layer], gate_wb[layer], gla_gate_b[layer]), batch, t)
        c_out = _linattn("ret", proj, ret_col0, gain[:, diff_w + gla_w:],
                         (cos_l, sin_l, ret_logit[layer]), batch, t)
        x2 = _outproj(x2, a_out, b_out, c_out, w_o, layer)
        act = _ffn_up(x2, ln2_g[layer][None, :], w_up, conv_w[layer], conv_b[layer][None, :], t, layer)
        x2 = _ffn_down(x2, act, w_down16, layer)
    return _rmsnorm(x2, final_g[None, :]).reshape(batch, t, d)
```

```python
import functools
import math

import jax
import jax.numpy as jnp
from jax import lax
from jax.experimental import pallas as pl
from jax.experimental.pallas import tpu as pltpu

F32 = jnp.float32
BF16 = jnp.bfloat16

HEAD_V = 128
DIFF_HEADS = 8
PAIR_HEADS = 2
N_PAIRS = 2
HEAD_K = 64
GLA_GATE_RANK = 16
GLA_TAU = 16.0
ROPE_BASE = 10000.0
N_BUCKETS = 32
MAX_DISTANCE = 128
CONV_W = 3
EPS = 1e-6
LOG2E = math.log2(math.e)

LANES = 128
SUBLANES = 8
BF16_ROWS = 16
VMEM_LIMIT_BYTES = 56 * 1024 * 1024

ATT_TILE = 256
ATT_QTILE = 1024
ATT_KSTAGE = 1024
GLA_CHUNK = 64
RET_CHUNK = 256
LIN_UNROLL = 8
MM_TILE_M = 1024
MM_TILE_N = 1024
FFN_TILE_N = 512
HALO = SUBLANES

NEG_BIG = -1e30

_NT = (((1,), (1,)), ((), ()))
_TN = (((0,), (0,)), ((), ()))


def _params(sem):
    return pltpu.CompilerParams(dimension_semantics=sem, vmem_limit_bytes=VMEM_LIMIT_BYTES)


def _dot(a, b):
    return jnp.dot(a, b, preferred_element_type=F32)


def _rms_rows(x, g):
    return x * lax.rsqrt(jnp.mean(x * x, axis=-1, keepdims=True) + EPS) * g


def _log_sigmoid(z):
    return jnp.minimum(z, 0.0) - jnp.log(1.0 + jnp.exp(-jnp.abs(z)))


def _silu(z):
    return z * (1.0 / (1.0 + jnp.exp(-z)))


def _inproj_kernel(x_ref, g_ref, w_ref, wga_ref, cs_ref, proj_ref, ga_ref, h_scr):
    @pl.when(pl.program_id(1) == 0)
    def _():
        hb = _rms_rows(x_ref[...], g_ref[...]).astype(BF16)
        h_scr[...] = hb
        ga_ref[...] = _dot(hb, wga_ref[...])

    proj_ref[...] = (_dot(h_scr[...], w_ref[...]) * cs_ref[...]).astype(BF16)


def _inproj(x2, g, w, wga, colscale, layer):
    m, d = x2.shape
    n = w.shape[-1]
    tm, tn = min(MM_TILE_M, m), min(MM_TILE_N, n)
    return pl.pallas_call(
        _inproj_kernel,
        grid=(m // tm, n // tn),
        in_specs=[
            pl.BlockSpec((tm, d), lambda i, j: (i, 0)),
            pl.BlockSpec((1, d), lambda i, j: (0, 0)),
            pl.BlockSpec((None, d, tn), lambda i, j: (layer, 0, j)),
            pl.BlockSpec((None, d, LANES), lambda i, j: (layer, 0, 0)),
            pl.BlockSpec((1, tn), lambda i, j: (0, j)),
        ],
        out_specs=[
            pl.BlockSpec((tm, tn), lambda i, j: (i, j)),
            pl.BlockSpec((tm, LANES), lambda i, j: (i, 0)),
        ],
        out_shape=[jax.ShapeDtypeStruct((m, n), BF16), jax.ShapeDtypeStruct((m, LANES), F32)],
        scratch_shapes=[pltpu.VMEM((tm, d), BF16)],
        compiler_params=_params(("parallel", "arbitrary")),
        name="inproj",
    )(x2, g, w, wga, colscale)


def _bias_tiles_kernel(rb_ref, bucket_ref, out_ref):
    h = pl.program_id(0)
    bucket = bucket_ref[...]
    acc = jnp.zeros(bucket.shape, F32)
    for b in range(N_BUCKETS):
        acc = jnp.where(bucket == b, rb_ref[b, h], acc)
    out_ref[...] = acc * LOG2E


def _bias_tiles(rel_bias, bucket):
    nd = bucket.shape[0]
    return pl.pallas_call(
        _bias_tiles_kernel,
        grid=(DIFF_HEADS,),
        in_specs=[
            pl.BlockSpec(memory_space=pltpu.SMEM),
            pl.BlockSpec((nd, ATT_TILE, ATT_TILE), lambda h: (0, 0, 0)),
        ],
        out_specs=pl.BlockSpec((None, nd, ATT_TILE, ATT_TILE), lambda h: (h, 0, 0, 0)),
        out_shape=jax.ShapeDtypeStruct((DIFF_HEADS, nd, ATT_TILE, ATT_TILE), F32),
        compiler_params=_params(("arbitrary",)),
        name="bias_tiles",
    )(rel_bias, bucket)


def _t5_bucket(rel):
    half = N_BUCKETS // 2
    max_exact = half // 2
    ret = jnp.where(rel > 0, half, 0)
    n = jnp.abs(rel)
    nf = jnp.maximum(n, 1).astype(F32)
    large = max_exact + (jnp.log(nf / max_exact) / math.log(MAX_DISTANCE / max_exact)
                         * (half - max_exact)).astype(jnp.int32)
    large = jnp.minimum(large, half - 1)
    return ret + jnp.where(n < max_exact, n, large)


ACC_ROWS = HEAD_V + BF16_ROWS


def _attn_kernel(q_ref, k_ref, v_ref, bias_ref, lam_ref, gain_ref, o_ref,
                 vt_scr, qh_scr, s_scr, p_scr, mloc_scr, alpha_scr, m_scr, acc_scr,
                 *, lam_init, n_stages, stage_keys, q_tile_rows):
    tile = ATT_TILE
    tq, ks = q_tile_rows, stage_keys
    n_qb = tq // tile
    n_kb = ks // tile
    ones_rows = (lax.broadcasted_iota(jnp.int32, (BF16_ROWS, ks), 0) == 0).astype(BF16)
    for c in range(n_stages):
        for kb in range(n_kb):
            vc = v_ref[c * ks + kb * tile:c * ks + (kb + 1) * tile, :].astype(F32)
            vt_scr[c, 0:HEAD_V, kb * tile:(kb + 1) * tile] = vc.T.astype(BF16)
        vt_scr[c, HEAD_V:ACC_ROWS, :] = ones_rows

    lp = lam_ref[...]
    lam = (jnp.exp(jnp.sum(lp[0:1] * lp[1:2], axis=-1, keepdims=True))
           - jnp.exp(jnp.sum(lp[2:3] * lp[3:4], axis=-1, keepdims=True)) + lam_init)
    first_half = lax.broadcasted_iota(jnp.int32, (1, 2 * HEAD_K), 1) < HEAD_K
    gain = gain_ref[...] * (1.0 - lam_init)

    def tile_rows(qi):
        return pl.ds(pl.multiple_of(qi * tq, tq), tq)

    def start_tile(qi):
        q = q_ref[tile_rows(qi), :]
        zero = jnp.zeros_like(q)
        qh_scr[0] = jnp.where(first_half, q, zero)
        qh_scr[1] = jnp.where(first_half, zero, q)
        for half in range(2):
            m_scr[half] = jnp.full((1, tq), NEG_BIG, F32)
            acc_scr[half, :, 0:tq] = jnp.zeros((ACC_ROWS, tq), F32)
        stage_a(qi, 0)

    def stage_a(qi, c):
        kc = k_ref[pl.ds(pl.multiple_of(c * ks, ks), ks), :]
        for qb in range(n_qb):
            cols = slice(qb * tile, (qb + 1) * tile)
            s = [lax.dot_general(kc, qh_scr[half, cols, :], _NT, preferred_element_type=F32)
                 for half in range(2)]
            mloc = [None, None]
            for kb in range(n_kb):
                krows = slice(kb * tile, (kb + 1) * tile)
                bias = bias_ref[jnp.clip((c * n_kb + kb) - (qi * n_qb + qb), -2, 2) + 2]
                for half in range(2):
                    sb = s[half][krows] + bias
                    s_scr[half, krows, cols] = sb
                    bmax = jnp.max(sb, axis=0, keepdims=True)
                    mloc[half] = bmax if kb == 0 else jnp.maximum(mloc[half], bmax)
            for half in range(2):
                mloc_scr[half, :, cols] = mloc[half]

    def stage_b():
        for half in range(2):
            m_old = m_scr[half]
            m_new = jnp.maximum(m_old, mloc_scr[half])
            alpha_scr[half] = jnp.exp2(m_old - m_new)
            m_scr[half] = m_new
            p_scr[half, :, 0:tq] = jnp.exp2(s_scr[half, :, 0:tq] - m_new).astype(BF16)

    def stage_c(c):
        vt = vt_scr[c]
        for half in range(2):
            acc_scr[half, :, 0:tq] = (acc_scr[half, :, 0:tq] * alpha_scr[half]
                                      + _dot(vt, p_scr[half, :, 0:tq]))

    def finish_tile(qi):
        acc1, acc2 = acc_scr[0, :, 0:tq], acc_scr[1, :, 0:tq]
        o1 = acc1[0:HEAD_V] / acc1[HEAD_V:HEAD_V + 1]
        o2 = acc2[0:HEAD_V] / acc2[HEAD_V:HEAD_V + 1]
        ot = o1 - lam * o2
        ot = ot * lax.rsqrt(jnp.mean(ot * ot, axis=0, keepdims=True) + EPS)
        o_ref[tile_rows(qi), :] = (ot.T * gain).astype(BF16)

    def q_tile(qi, has_next):
        stage_b()
        stage_a(qi, 1)

        def body(c, _):
            stage_c(c)
            stage_b()
            stage_a(qi, c + 2)
            return 0

        lax.fori_loop(0, n_stages - 2, body, 0)
        stage_c(n_stages - 2)
        stage_b()
        stage_c(n_stages - 1)
        finish_tile(qi)
        if has_next:
            start_tile(qi + 1)

    n_q = (n_stages * ks) // tq
    start_tile(0)

    def q_loop(qi, _):
        q_tile(qi, True)
        return 0

    lax.fori_loop(0, n_q - 1, q_loop, 0)
    q_tile(n_q - 1, False)


def _diff_attention(proj, bias_tiles, lam_params, gain, lam_init, batch, t):
    nd = bias_tiles.shape[1]
    tile = ATT_TILE
    tq = min(ATT_QTILE, t)
    ks = min(ATT_KSTAGE, t // 2)
    n_stages = t // ks
    return pl.pallas_call(
        functools.partial(_attn_kernel, lam_init=lam_init, n_stages=n_stages, stage_keys=ks, q_tile_rows=tq),
        grid=(batch, DIFF_HEADS),
        in_specs=[
            pl.BlockSpec((t, HEAD_V), lambda b, h: (b, h)),
            pl.BlockSpec((t, HEAD_V), lambda b, h: (b, DIFF_HEADS + h)),
            pl.BlockSpec((t, HEAD_V), lambda b, h: (b, 2 * DIFF_HEADS + h)),
            pl.BlockSpec((None, nd, tile, tile), lambda b, h: (h, 0, 0, 0)),
            pl.BlockSpec(lam_params.shape, lambda b, h: (0, 0)),
            pl.BlockSpec((1, HEAD_V), lambda b, h: (0, h)),
        ],
        out_specs=pl.BlockSpec((t, HEAD_V), lambda b, h: (b, h)),
        out_shape=jax.ShapeDtypeStruct((batch * t, DIFF_HEADS * HEAD_V), BF16),
        scratch_shapes=[
            pltpu.VMEM((n_stages, ACC_ROWS, ks), BF16),
            pltpu.VMEM((2, tq, 2 * HEAD_K), BF16),
            pltpu.VMEM((2, ks, tq + LANES), F32),
            pltpu.VMEM((2, ks, tq + LANES), BF16),
            pltpu.VMEM((2, 1, tq), F32),
            pltpu.VMEM((2, 1, tq), F32),
            pltpu.VMEM((2, 1, tq), F32),
            pltpu.VMEM((2, ACC_ROWS, tq + LANES), F32),
        ],
        compiler_params=_params(("parallel", "arbitrary")),
        name="diff_attn",
    )(proj, proj, proj, bias_tiles, lam_params, gain)


def _linattn_kernel(*refs, mode, chunk, n_chunks):
    if mode == "gla":
        (q_ref, k_ref, v_ref, gate_ref, gain_ref, ga_ref, wf_ref, wb_ref, gb_ref,
         o_ref, b_scr, einc_scr, lab_scr, kvf_scr, rnext_scr) = refs
    else:
        (q_ref, k_ref, v_ref, gate_ref, gain_ref, cos_ref, sin_ref, logit_ref,
         o_ref, kx_scr, kvf_scr, rnext_scr) = refs
    c = chunk
    kw = PAIR_HEADS * HEAD_K
    vw = PAIR_HEADS * HEAD_V

    lane_k = lax.broadcasted_iota(jnp.int32, (1, kw), 1)
    head0_k = lane_k < HEAD_K
    st_row = lax.broadcasted_iota(jnp.int32, (vw, kw), 0)
    st_lane = lax.broadcasted_iota(jnp.int32, (vw, kw), 1)
    same_head = (st_row < HEAD_V) == (st_lane < HEAD_K)
    ri = lax.broadcasted_iota(jnp.int32, (c, c), 0)
    ci = lax.broadcasted_iota(jnp.int32, (c, c), 1)
    lower, upper = ri >= ci, ri <= ci

    def chunk_rows(n):
        return pl.ds(pl.multiple_of(n * c, c), c)

    if mode == "ret":
        pr = lax.broadcasted_iota(jnp.int32, (kw, kw), 0)
        pc = lax.broadcasted_iota(jnp.int32, (kw, kw), 1)
        swap = (pr == (pc ^ (HEAD_K // 2))).astype(BF16)

        def rotary(x_bf16, rows):
            xr = _dot(x_bf16, swap)
            return x_bf16.astype(F32) * cos_ref[rows, :] + xr * sin_ref[rows, :]

        lg = _log_sigmoid(logit_ref[...])
        lgf, lgb = lg[0:1], lg[1:2]
        pos = lax.broadcasted_iota(jnp.int32, (c, kw), 0).astype(F32)
        b_const, e_const = (pos + 1.0) * lgf, pos * lgb
        totf_const, totb_const = float(c) * lgf, float(c) * lgb
        rel = (ri - ci).astype(F32)
        dmask = []
        for h in range(PAIR_HEADS):
            lf = lgf[:, h * HEAD_K:h * HEAD_K + 1]
            lb = lgb[:, h * HEAD_K:h * HEAD_K + 1]
            dmask.append(jnp.where(lower, jnp.exp(lf * jnp.maximum(rel, 0.0)), 0.0)
                         + jnp.where(upper, jnp.exp(lb * jnp.maximum(-rel, 0.0)), 0.0))
    else:
        ga = ga_ref[...].astype(BF16)
        b_scr[...] = _log_sigmoid(_dot(ga, wf_ref[...]) + gb_ref[0:1, :]) * (1.0 / GLA_TAU)
        lab_scr[...] = _log_sigmoid(_dot(ga, wb_ref[...]) + gb_ref[1:2, :]) * (1.0 / GLA_TAU)
        row_in_chunk = lax.broadcasted_iota(jnp.int32, (c, kw), 0)

        def chunk_cumsum(x):
            shift = 1
            while shift < c:
                x = x + jnp.where(row_in_chunk >= shift, pltpu.roll(x, shift, 0), 0.0)
                shift *= 2
            return x

    def phase1(i, r_state):
        n = n_chunks - 1 - i
        rows = chunk_rows(n)
        v = v_ref[rows, :]
        if mode == "gla":
            k = k_ref[rows, :].astype(F32)
            lab = lab_scr[rows, :]
            b, einc = chunk_cumsum(b_scr[rows, :]), chunk_cumsum(lab)
            b_scr[rows, :] = b
            einc_scr[rows, :] = einc
            e, totf, totb = einc - lab, b[c - 1:c, :], einc[c - 1:c, :]
        else:
            k = rotary(k_ref[rows, :], rows)
            kx_scr[rows, :] = k
            b, e, totf, totb = b_const, e_const, totf_const, totb_const
        kf = (k * jnp.exp(totf - b)).astype(BF16)
        kb = (k * jnp.exp(e)).astype(BF16)
        kvf = lax.dot_general(v, kf, _TN, preferred_element_type=F32)
        kvb = lax.dot_general(v, kb, _TN, preferred_element_type=F32)
        kvf_scr[n] = jnp.where(same_head, kvf, 0.0)
        rnext_scr[n] = r_state.astype(BF16)
        return jnp.exp(totb) * r_state + jnp.where(same_head, kvb, 0.0)

    lax.fori_loop(0, n_chunks, phase1, jnp.zeros((vw, kw), F32), unroll=LIN_UNROLL)

    def phase2(n, s_state):
        rows = chunk_rows(n)
        v = v_ref[rows, :]
        if mode == "gla":
            q = q_ref[rows, :].astype(F32)
            k = k_ref[rows, :].astype(F32)
            b, einc = b_scr[rows, :], einc_scr[rows, :]
            e, totf, totb = einc - lab_scr[rows, :], b[c - 1:c, :], einc[c - 1:c, :]
        else:
            q = rotary(q_ref[rows, :], rows)
            k = kx_scr[rows, :]
            b, e, totf, totb = b_const, e_const, totf_const, totb_const

        qf = (q * jnp.exp(b)).astype(BF16)
        qb = (q * jnp.exp(totb - e)).astype(BF16)
        inter = (lax.dot_general(qf, s_state.astype(BF16), _NT, preferred_element_type=F32)
                 + lax.dot_general(qb, rnext_scr[n], _NT, preferred_element_type=F32))

        if mode == "gla":
            bm = b[c // 2 - 1:c // 2, :]
            em = e[c // 2:c // 2 + 1, :]
            qtf, ktf = q * jnp.exp(b - bm), (k * jnp.exp(bm - b)).astype(BF16)
            qtb, ktb = q * jnp.exp(em - e), (k * jnp.exp(e - em)).astype(BF16)
        else:
            kb16 = k.astype(BF16)
        outs = []
        for h in range(PAIR_HEADS):
            mh = head0_k if h == 0 else jnp.logical_not(head0_k)
            if mode == "gla":
                sf = lax.dot_general(jnp.where(mh, qtf, 0.0).astype(BF16), ktf, _NT, preferred_element_type=F32)
                sb = lax.dot_general(jnp.where(mh, qtb, 0.0).astype(BF16), ktb, _NT, preferred_element_type=F32)
                sc = jnp.where(lower, sf, 0.0) + jnp.where(upper, sb, 0.0)
            else:
                sc = lax.dot_general(jnp.where(mh, q, 0.0).astype(BF16), kb16, _NT,
                                     preferred_element_type=F32) * dmask[h]
            outs.append(_dot(sc.astype(BF16), v[:, h * HEAD_V:(h + 1) * HEAD_V]))

        gate = gate_ref[rows, :].astype(F32)
        for h in range(PAIR_HEADS):
            cols = slice(h * HEAD_V, (h + 1) * HEAD_V)
            o = outs[h] + inter[:, cols]
            o = _rms_rows(o, gain_ref[:, cols]) * _silu(gate[:, cols])
            o_ref[rows, cols] = o.astype(BF16)

        kvf = kvf_scr[n]
        return jnp.exp(totf) * s_state + kvf

    lax.fori_loop(0, n_chunks, phase2, jnp.zeros((vw, kw), F32), unroll=LIN_UNROLL)


def _linattn(mode, proj, col0, gain, extra, batch, t):
    chunk = GLA_CHUNK if mode == "gla" else RET_CHUNK
    n_chunks = t // chunk
    kw, vw = PAIR_HEADS * HEAD_K, PAIR_HEADS * HEAD_V
    v0 = (col0 + 2 * N_PAIRS) // 2
    g0 = v0 + N_PAIRS
    in_specs = [
        pl.BlockSpec((t, kw), lambda b, p: (b, col0 + p)),
        pl.BlockSpec((t, kw), lambda b, p: (b, col0 + N_PAIRS + p)),
        pl.BlockSpec((t, vw), lambda b, p: (b, v0 + p)),
        pl.BlockSpec((t, vw), lambda b, p: (b, g0 + p)),
        pl.BlockSpec((1, vw), lambda b, p: (0, p)),
    ]
    args = [proj, proj, proj, proj, gain]
    state_scratch = [pltpu.VMEM((n_chunks, vw, kw), F32), pltpu.VMEM((n_chunks, vw, kw), BF16)]
    if mode == "gla":
        ga, wf, wb, gb = extra
        in_specs += [
            pl.BlockSpec((t, LANES), lambda b, p: (b, 0)),
            pl.BlockSpec((LANES, kw), lambda b, p: (0, p)),
            pl.BlockSpec((LANES, kw), lambda b, p: (0, p)),
            pl.BlockSpec((2, kw), lambda b, p: (0, p)),
        ]
        args += [ga, wf, wb, gb]
        scratch = [pltpu.VMEM((t, kw), F32)] * 3 + state_scratch
    else:
        cos, sin, logit = extra
        in_specs += [
            pl.BlockSpec((t, kw), lambda b, p: (0, 0)),
            pl.BlockSpec((t, kw), lambda b, p: (0, 0)),
            pl.BlockSpec((2, kw), lambda b, p: (0, p)),
        ]
        args += [cos, sin, logit]
        scratch = [pltpu.VMEM((t, kw), F32)] + state_scratch
    return pl.pallas_call(
        functools.partial(_linattn_kernel, mode=mode, chunk=chunk, n_chunks=n_chunks),
        grid=(batch, N_PAIRS),
        in_specs=in_specs,
        out_specs=pl.BlockSpec((t, vw), lambda b, p: (b, p)),
        out_shape=jax.ShapeDtypeStruct((batch * t, N_PAIRS * vw), BF16),
        scratch_shapes=scratch,
        compiler_params=_params(("parallel", "arbitrary")),
        name="linattn_" + mode,
    )(*args)


def _outproj_kernel(x_ref, a_ref, b_ref, c_ref, wa_ref, wb_ref, wc_ref, o_ref):
    o_ref[...] = (x_ref[...] + _dot(a_ref[...], wa_ref[...].astype(BF16))
                  + _dot(b_ref[...], wb_ref[...].astype(BF16)) + _dot(c_ref[...], wc_ref[...].astype(BF16)))


def _outproj(x2, a, b, c, w_o, layer):
    m, d = x2.shape
    tm, tn = min(MM_TILE_M // 2, m), d
    wa_rows, wb_rows = a.shape[1], b.shape[1]
    nb = wa_rows // wb_rows
    return pl.pallas_call(
        _outproj_kernel,
        grid=(m // tm, d // tn),
        in_specs=[
            pl.BlockSpec((tm, tn), lambda i, j: (i, j)),
            pl.BlockSpec((tm, wa_rows), lambda i, j: (i, 0)),
            pl.BlockSpec((tm, wb_rows), lambda i, j: (i, 0)),
            pl.BlockSpec((tm, wb_rows), lambda i, j: (i, 0)),
            pl.BlockSpec((None, wa_rows, tn), lambda i, j: (layer, 0, j), pipeline_mode=pl.Buffered(1)),
            pl.BlockSpec((None, wb_rows, tn), lambda i, j: (layer, nb, j), pipeline_mode=pl.Buffered(1)),
            pl.BlockSpec((None, wb_rows, tn), lambda i, j: (layer, nb + 1, j), pipeline_mode=pl.Buffered(1)),
        ],
        out_specs=pl.BlockSpec((tm, tn), lambda i, j: (i, j)),
        out_shape=jax.ShapeDtypeStruct((m, d), F32),
        compiler_params=_params(("parallel", "arbitrary")),
        name="outproj",
    )(x2, a, b, c, w_o, w_o, w_o)


def _ffn_up_kernel(x_ref, xp_ref, xn_ref, g_ref, wa_ref, wg_ref, cwa_ref, cwg_ref, cba_ref, cbg_ref,
                   o_ref, h_scr, *, tiles_per_seq):
    i = pl.program_id(0)
    tm = x_ref.shape[0]

    @pl.when(pl.program_id(1) == 0)
    def _():
        g = g_ref[...]
        keep_prev = (i % tiles_per_seq != 0).astype(F32)
        keep_next = ((i + 1) % tiles_per_seq != 0).astype(F32)
        h_scr[0:HALO, :] = (_rms_rows(xp_ref[...], g) * keep_prev).astype(BF16)
        h_scr[HALO:HALO + tm, :] = _rms_rows(x_ref[...], g).astype(BF16)
        h_scr[HALO + tm:, :] = (_rms_rows(xn_ref[...], g) * keep_next).astype(BF16)

    h = h_scr[...]
    n_ext = tm + 2 * HALO

    def conv(w_ref, cw_ref, cb_ref):
        u = _dot(h, w_ref[...].astype(BF16))
        prev = pltpu.roll(u, 1, 0)[HALO:HALO + tm]
        nxt = pltpu.roll(u, n_ext - 1, 0)[HALO:HALO + tm]
        return prev * cw_ref[0:1, :] + u[HALO:HALO + tm] * cw_ref[1:2, :] + nxt * cw_ref[2:3, :] + cb_ref[...]

    gate = _silu(conv(wg_ref, cwg_ref, cbg_ref))
    o_ref[...] = (conv(wa_ref, cwa_ref, cba_ref) * gate).astype(BF16)


def _ffn_up(x2, g, w_up, conv_w, conv_b, t, layer):
    m, d = x2.shape
    d_ff = w_up.shape[-1] // 2
    tm = min(MM_TILE_M, t)
    tn = FFN_TILE_N
    nj = d_ff // tn
    blocks_per_tile = tm // HALO
    last_block = m // HALO - 1
    return pl.pallas_call(
        functools.partial(_ffn_up_kernel, tiles_per_seq=t // tm),
        grid=(m // tm, nj),
        in_specs=[
            pl.BlockSpec((tm, d), lambda i, j: (i, 0)),
            pl.BlockSpec((HALO, d), lambda i, j: (jnp.maximum(i * blocks_per_tile - 1, 0), 0)),
            pl.BlockSpec((HALO, d), lambda i, j: (jnp.minimum((i + 1) * blocks_per_tile, last_block), 0)),
            pl.BlockSpec((1, d), lambda i, j: (0, 0)),
            pl.BlockSpec((None, d, tn), lambda i, j: (layer, 0, j)),
            pl.BlockSpec((None, d, tn), lambda i, j: (layer, 0, nj + j)),
            pl.BlockSpec((CONV_W, tn), lambda i, j: (0, j)),
            pl.BlockSpec((CONV_W, tn), lambda i, j: (0, nj + j)),
            pl.BlockSpec((1, tn), lambda i, j: (0, j)),
            pl.BlockSpec((1, tn), lambda i, j: (0, nj + j)),
        ],
        out_specs=pl.BlockSpec((tm, tn), lambda i, j: (i, j)),
        out_shape=jax.ShapeDtypeStruct((m, d_ff), BF16),
        scratch_shapes=[pltpu.VMEM((tm + 2 * HALO, d), BF16)],
        compiler_params=_params(("parallel", "arbitrary")),
        name="ffn_up",
    )(x2, x2, x2, g, w_up, w_up, conv_w, conv_w, conv_b, conv_b)


def _ffn_down_kernel(x_ref, a_ref, w_ref, o_ref):
    o_ref[...] = x_ref[...] + _dot(a_ref[...], w_ref[...])


def _ffn_down(x2, act, w_down, layer):
    m, d = x2.shape
    k = act.shape[1]
    tm, tn = min(MM_TILE_M, m), FFN_TILE_N
    return pl.pallas_call(
        _ffn_down_kernel,
        grid=(m // tm, d // tn),
        in_specs=[
            pl.BlockSpec((tm, tn), lambda i, j: (i, j)),
            pl.BlockSpec((tm, k), lambda i, j: (i, 0)),
            pl.BlockSpec((None, k, tn), lambda i, j: (layer, 0, j)),
        ],
        out_specs=pl.BlockSpec((tm, tn), lambda i, j: (i, j)),
        out_shape=jax.ShapeDtypeStruct((m, d), F32),
        compiler_params=_params(("parallel", "arbitrary")),
        name="ffn_down",
    )(x2, act, w_down)


def _rmsnorm_kernel(x_ref, g_ref, o_ref):
    o_ref[...] = _rms_rows(x_ref[...], g_ref[...])


def _rmsnorm(x2, g):
    m, d = x2.shape
    tm = min(MM_TILE_M, m)
    return pl.pallas_call(
        _rmsnorm_kernel,
        grid=(m // tm,),
        in_specs=[pl.BlockSpec((tm, d), lambda i: (i, 0)), pl.BlockSpec((1, d), lambda i: (0, 0))],
        out_specs=pl.BlockSpec((tm, d), lambda i: (i, 0)),
        out_shape=jax.ShapeDtypeStruct((m, d), F32),
        compiler_params=_params(("parallel",)),
        name="final_rmsnorm",
    )(x2, g)


def _rotary_tables(t):
    half = HEAD_K // 2
    inv = 1.0 / (ROPE_BASE ** jnp.linspace(0.0, 1.0, half, dtype=F32))
    ang = jnp.arange(t, dtype=jnp.int32).astype(F32)[:, None] * inv[None, :]
    cos, sin = jnp.cos(ang), jnp.sin(ang)
    cos_l = jnp.tile(cos, (1, 2 * PAIR_HEADS))
    sin_l = jnp.tile(jnp.concatenate([-sin, sin], axis=-1), (1, PAIR_HEADS))
    return cos_l, sin_l


def kernel(x, ln1_g, w_in, diff_lambda, rel_bias, gla_gate_w, gla_gate_b, ret_decay_logit, head_gain,
           w_o, ln2_g, w_up, conv_w, conv_b, w_down, final_g):
    batch, t, d = x.shape
    depth = w_in.shape[0]
    diff_w = DIFF_HEADS * HEAD_V
    gla_w = 2 * N_PAIRS * HEAD_V
    gla_k = 2 * N_PAIRS * HEAD_K
    ga0 = 3 * diff_w + 2 * gla_k + 2 * gla_w
    ga1 = ga0 + 2 * GLA_GATE_RANK
    gla_col0 = 3 * diff_w // LANES
    ret_col0 = ga0 // LANES

    w_main = jnp.concatenate([w_in[:, :, :ga0], w_in[:, :, ga1:]], axis=-1).astype(BF16)
    w_ga = jnp.pad(w_in[:, :, ga0:ga1], ((0, 0), (0, 0), (0, LANES - 2 * GLA_GATE_RANK))).astype(BF16)
    w_down16 = w_down.astype(BF16)
    k_scale = HEAD_K ** -0.5
    colscale = jnp.ones((1, w_main.shape[-1]), F32)
    colscale = colscale.at[:, :diff_w].set(k_scale * LOG2E)
    colscale = colscale.at[:, 3 * diff_w:3 * diff_w + gla_k].set(k_scale)
    colscale = colscale.at[:, ga0 + gla_k:ga0 + 2 * gla_k].set(k_scale)
    r = GLA_GATE_RANK
    gate_wf = jnp.pad(gla_gate_w[:, 0], ((0, 0), (0, LANES - r), (0, 0))).astype(BF16)
    gate_wb = jnp.pad(gla_gate_w[:, 1], ((0, 0), (r, LANES - 2 * r), (0, 0))).astype(BF16)
    ret_logit = jnp.repeat(ret_decay_logit, HEAD_K, axis=-1)

    rel = (jnp.arange(ATT_TILE, dtype=jnp.int32)[:, None] - jnp.arange(ATT_TILE, dtype=jnp.int32)[None, :])
    rel = rel[None] + ATT_TILE * (jnp.arange(5, dtype=jnp.int32) - 2)[:, None, None]
    bias_tiles = _bias_tiles(rel_bias.astype(F32), _t5_bucket(rel))
    cos_l, sin_l = _rotary_tables(t)

    x2 = x.reshape(batch * t, d)
    for layer in range(depth):
        lam_init = 0.8 - 0.6 * math.exp(-0.3 * layer)
        gain = head_gain[layer][None, :]
        proj, ga = _inproj(x2, ln1_g[layer][None, :], w_main, w_ga, colscale, layer)
        a_out = _diff_attention(proj, bias_tiles, diff_lambda[layer], gain[:, :diff_w], lam_init, batch, t)
        b_out = _linattn("gla", proj, gla_col0, gain[:, diff_w:diff_w + gla_w],
                         (ga, gate_wf[layer], gate_wb[layer], gla_gate_b[layer]), batch, t)
        c_out = _linattn("ret", proj, ret_col0, gain[:, diff_w + gla_w:],
                         (cos_l, sin_l, ret_logit[layer]), batch, t)
        x2 = _outproj(x2, a_out, b_out, c_out, w_o, layer)
        act = _ffn_up(x2, ln2_g[layer][None, :], w_up, conv_w[layer], conv_b[layer][None, :], t, layer)
        x2 = _ffn_down(x2, act, w_down16, layer)
    return _rmsnorm(x2, final_g[None, :]).reshape(batch, t, d)
```

```python
import functools
import math

import jax
import jax.numpy as jnp
from jax import lax
from jax.experimental import pallas as pl
from jax.experimental.pallas import tpu as pltpu

F32 = jnp.float32
BF16 = jnp.bfloat16

HEAD_V = 128
DIFF_HEADS = 8
PAIR_HEADS = 2
N_PAIRS = 2
HEAD_K = 64
GLA_GATE_RANK = 16
GLA_TAU = 16.0
ROPE_BASE = 10000.0
N_BUCKETS = 32
MAX_DISTANCE = 128
CONV_W = 3
EPS = 1e-6
LOG2E = math.log2(math.e)

LANES = 128
SUBLANES = 8
BF16_ROWS = 16
VMEM_LIMIT_BYTES = 56 * 1024 * 1024

ATT_TILE = 256
ATT_QTILE = 1024
ATT_KSTAGE = 1024
GLA_CHUNK = 64
RET_CHUNK = 256
LIN_BLOCK_ROWS = 256
LIN_UNROLL = 8
MM_TILE_M = 1024
MM_TILE_N = 1024
FFN_TILE_N = 512
HALO = SUBLANES

NEG_BIG = -1e30

_NT = (((1,), (1,)), ((), ()))
_TN = (((0,), (0,)), ((), ()))


def _params(sem):
    return pltpu.CompilerParams(dimension_semantics=sem, vmem_limit_bytes=VMEM_LIMIT_BYTES)


def _dot(a, b):
    return jnp.dot(a, b, preferred_element_type=F32)


def _rms_rows(x, g):
    return x * lax.rsqrt(jnp.mean(x * x, axis=-1, keepdims=True) + EPS) * g


def _log_sigmoid(z):
    return jnp.minimum(z, 0.0) - jnp.log(1.0 + jnp.exp(-jnp.abs(z)))


def _silu(z):
    return z * (1.0 / (1.0 + jnp.exp(-z)))


def _inproj_kernel(x_ref, g_ref, w_ref, wga_ref, cs_ref, proj_ref, ga_ref, h_scr):
    @pl.when(pl.program_id(1) == 0)
    def _():
        hb = _rms_rows(x_ref[...], g_ref[...]).astype(BF16)
        h_scr[...] = hb
        ga_ref[...] = _dot(hb, wga_ref[...])

    proj_ref[...] = (_dot(h_scr[...], w_ref[...]) * cs_ref[...]).astype(BF16)


def _inproj(x2, g, w, wga, colscale, layer):
    m, d = x2.shape
    n = w.shape[-1]
    tm, tn = min(MM_TILE_M, m), min(MM_TILE_N, n)
    return pl.pallas_call(
        _inproj_kernel,
        grid=(m // tm, n // tn),
        in_specs=[
            pl.BlockSpec((tm, d), lambda i, j: (i, 0)),
            pl.BlockSpec((1, d), lambda i, j: (0, 0)),
            pl.BlockSpec((None, d, tn), lambda i, j: (layer, 0, j)),
            pl.BlockSpec((None, d, LANES), lambda i, j: (layer, 0, 0)),
            pl.BlockSpec((1, tn), lambda i, j: (0, j)),
        ],
        out_specs=[
            pl.BlockSpec((tm, tn), lambda i, j: (i, j)),
            pl.BlockSpec((tm, LANES), lambda i, j: (i, 0)),
        ],
        out_shape=[jax.ShapeDtypeStruct((m, n), BF16), jax.ShapeDtypeStruct((m, LANES), F32)],
        scratch_shapes=[pltpu.VMEM((tm, d), BF16)],
        compiler_params=_params(("parallel", "arbitrary")),
        name="inproj",
    )(x2, g, w, wga, colscale)


def _bias_tiles_kernel(rb_ref, bucket_ref, out_ref):
    h = pl.program_id(0)
    bucket = bucket_ref[...]
    acc = jnp.zeros(bucket.shape, F32)
    for b in range(N_BUCKETS):
        acc = jnp.where(bucket == b, rb_ref[b, h], acc)
    out_ref[...] = acc * LOG2E


def _bias_tiles(rel_bias, bucket):
    nd = bucket.shape[0]
    return pl.pallas_call(
        _bias_tiles_kernel,
        grid=(DIFF_HEADS,),
        in_specs=[
            pl.BlockSpec(memory_space=pltpu.SMEM),
            pl.BlockSpec((nd, ATT_TILE, ATT_TILE), lambda h: (0, 0, 0)),
        ],
        out_specs=pl.BlockSpec((None, nd, ATT_TILE, ATT_TILE), lambda h: (h, 0, 0, 0)),
        out_shape=jax.ShapeDtypeStruct((DIFF_HEADS, nd, ATT_TILE, ATT_TILE), F32),
        compiler_params=_params(("arbitrary",)),
        name="bias_tiles",
    )(rel_bias, bucket)


def _t5_bucket(rel):
    half = N_BUCKETS // 2
    max_exact = half // 2
    ret = jnp.where(rel > 0, half, 0)
    n = jnp.abs(rel)
    nf = jnp.maximum(n, 1).astype(F32)
    large = max_exact + (jnp.log(nf / max_exact) / math.log(MAX_DISTANCE / max_exact)
                         * (half - max_exact)).astype(jnp.int32)
    large = jnp.minimum(large, half - 1)
    return ret + jnp.where(n < max_exact, n, large)


ACC_ROWS = HEAD_V + BF16_ROWS


def _attn_kernel(q_ref, k_ref, v_ref, bias_ref, lam_ref, gain_ref, o_ref,
                 vt_scr, qh_scr, s_scr, p_scr, mloc_scr, alpha_scr, m_scr, acc_scr,
                 *, lam_init, n_stages, stage_keys, q_tile_rows):
    tile = ATT_TILE
    tq, ks = q_tile_rows, stage_keys
    n_qb = tq // tile
    n_kb = ks // tile
    ones_rows = (lax.broadcasted_iota(jnp.int32, (BF16_ROWS, ks), 0) == 0).astype(BF16)
    for c in range(n_stages):
        for kb in range(n_kb):
            vc = v_ref[c * ks + kb * tile:c * ks + (kb + 1) * tile, :].astype(F32)
            vt_scr[c, 0:HEAD_V, kb * tile:(kb + 1) * tile] = vc.T.astype(BF16)
        vt_scr[c, HEAD_V:ACC_ROWS, :] = ones_rows

    lp = lam_ref[...]
    lam = (jnp.exp(jnp.sum(lp[0:1] * lp[1:2], axis=-1, keepdims=True))
           - jnp.exp(jnp.sum(lp[2:3] * lp[3:4], axis=-1, keepdims=True)) + lam_init)
    first_half = lax.broadcasted_iota(jnp.int32, (1, 2 * HEAD_K), 1) < HEAD_K
    gain = gain_ref[...] * (1.0 - lam_init)

    def tile_rows(qi):
        return pl.ds(pl.multiple_of(qi * tq, tq), tq)

    def start_tile(qi):
        q = q_ref[tile_rows(qi), :]
        zero = jnp.zeros_like(q)
        qh_scr[0] = jnp.where(first_half, q, zero)
        qh_scr[1] = jnp.where(first_half, zero, q)
        for half in range(2):
            m_scr[half] = jnp.full((1, tq), NEG_BIG, F32)
            acc_scr[half, :, 0:tq] = jnp.zeros((ACC_ROWS, tq), F32)
        stage_a(qi, 0)

    def stage_a(qi, c):
        kc = k_ref[pl.ds(pl.multiple_of(c * ks, ks), ks), :]
        for qb in range(n_qb):
            cols = slice(qb * tile, (qb + 1) * tile)
            s = [lax.dot_general(kc, qh_scr[half, cols, :], _NT, preferred_element_type=F32)
                 for half in range(2)]
            mloc = [None, None]
            for kb in range(n_kb):
                krows = slice(kb * tile, (kb + 1) * tile)
                bias = bias_ref[jnp.clip((c * n_kb + kb) - (qi * n_qb + qb), -2, 2) + 2]
                for half in range(2):
                    sb = s[half][krows] + bias
                    s_scr[half, krows, cols] = sb
                    bmax = jnp.max(sb, axis=0, keepdims=True)
                    mloc[half] = bmax if kb == 0 else jnp.maximum(mloc[half], bmax)
            for half in range(2):
                mloc_scr[half, :, cols] = mloc[half]

    def stage_b():
        for half in range(2):
            m_old = m_scr[half]
            m_new = jnp.maximum(m_old, mloc_scr[half])
            alpha_scr[half] = jnp.exp2(m_old - m_new)
            m_scr[half] = m_new
            p_scr[half, :, 0:tq] = jnp.exp2(s_scr[half, :, 0:tq] - m_new).astype(BF16)

    def stage_c(c):
        vt = vt_scr[c]
        for half in range(2):
            acc_scr[half, :, 0:tq] = (acc_scr[half, :, 0:tq] * alpha_scr[half]
                                      + _dot(vt, p_scr[half, :, 0:tq]))

    def finish_tile(qi):
        acc1, acc2 = acc_scr[0, :, 0:tq], acc_scr[1, :, 0:tq]
        o1 = acc1[0:HEAD_V] / acc1[HEAD_V:HEAD_V + 1]
        o2 = acc2[0:HEAD_V] / acc2[HEAD_V:HEAD_V + 1]
        ot = o1 - lam * o2
        ot = ot * lax.rsqrt(jnp.mean(ot * ot, axis=0, keepdims=True) + EPS)
        o_ref[tile_rows(qi), :] = (ot.T * gain).astype(BF16)

    def q_tile(qi, has_next):
        stage_b()
        stage_a(qi, 1)

        def body(c, _):
            stage_c(c)
            stage_b()
            stage_a(qi, c + 2)
            return 0

        lax.fori_loop(0, n_stages - 2, body, 0)
        stage_c(n_stages - 2)
        stage_b()
        stage_c(n_stages - 1)
        finish_tile(qi)
        if has_next:
            start_tile(qi + 1)

    n_q = (n_stages * ks) // tq
    start_tile(0)

    def q_loop(qi, _):
        q_tile(qi, True)
        return 0

    lax.fori_loop(0, n_q - 1, q_loop, 0)
    q_tile(n_q - 1, False)


def _diff_attention(proj, bias_tiles, lam_params, gain, lam_init, batch, t):
    nd = bias_tiles.shape[1]
    tile = ATT_TILE
    tq = min(ATT_QTILE, t)
    ks = min(ATT_KSTAGE, t // 2)
    n_stages = t // ks
    return pl.pallas_call(
        functools.partial(_attn_kernel, lam_init=lam_init, n_stages=n_stages, stage_keys=ks, q_tile_rows=tq),
        grid=(batch, DIFF_HEADS),
        in_specs=[
            pl.BlockSpec((t, HEAD_V), lambda b, h: (b, h)),
            pl.BlockSpec((t, HEAD_V), lambda b, h: (b, DIFF_HEADS + h)),
            pl.BlockSpec((t, HEAD_V), lambda b, h: (b, 2 * DIFF_HEADS + h)),
            pl.BlockSpec((None, nd, tile, tile), lambda b, h: (h, 0, 0, 0)),
            pl.BlockSpec(lam_params.shape, lambda b, h: (0, 0)),
            pl.BlockSpec((1, HEAD_V), lambda b, h: (0, h)),
        ],
        out_specs=pl.BlockSpec((t, HEAD_V), lambda b, h: (b, h)),
        out_shape=jax.ShapeDtypeStruct((batch * t, DIFF_HEADS * HEAD_V), BF16),
        scratch_shapes=[
            pltpu.VMEM((n_stages, ACC_ROWS, ks), BF16),
            pltpu.VMEM((2, tq, 2 * HEAD_K), BF16),
            pltpu.VMEM((2, ks, tq + LANES), F32),
            pltpu.VMEM((2, ks, tq + LANES), BF16),
            pltpu.VMEM((2, 1, tq), F32),
            pltpu.VMEM((2, 1, tq), F32),
            pltpu.VMEM((2, 1, tq), F32),
            pltpu.VMEM((2, ACC_ROWS, tq + LANES), F32),
        ],
        compiler_params=_params(("parallel", "arbitrary")),
        name="diff_attn",
    )(proj, proj, proj, bias_tiles, lam_params, gain)


def _linattn_kernel(*refs, mode, chunk, n_chunks):
    if mode == "gla":
        (q_ref, k_ref, v_ref, gate_ref, gain_ref, ga_ref, wf_ref, wb_ref, gb_ref,
         o_ref, b_scr, einc_scr, lab_scr, kvf_scr, rnext_scr) = refs
    else:
        (q_ref, k_ref, v_ref, gate_ref, gain_ref, cos_ref, sin_ref, logit_ref,
         o_ref, kx_scr, kvf_scr, rnext_scr) = refs
    c = chunk
    kw = PAIR_HEADS * HEAD_K
    vw = PAIR_HEADS * HEAD_V

    lane_k = lax.broadcasted_iota(jnp.int32, (1, kw), 1)
    head0_k = lane_k < HEAD_K
    st_row = lax.broadcasted_iota(jnp.int32, (vw, kw), 0)
    st_lane = lax.broadcasted_iota(jnp.int32, (vw, kw), 1)
    same_head = (st_row < HEAD_V) == (st_lane < HEAD_K)
    grp = max(1, LIN_BLOCK_ROWS // c)
    blk = grp * c
    ri = lax.broadcasted_iota(jnp.int32, (blk, blk), 0)
    ci = lax.broadcasted_iota(jnp.int32, (blk, blk), 1)
    same_chunk = (ri // c) == (ci // c)
    lower, upper = same_chunk & (ri >= ci), same_chunk & (ri <= ci)

    def chunk_rows(n):
        return pl.ds(pl.multiple_of(n * c, c), c)

    if mode == "ret":
        pr = lax.broadcasted_iota(jnp.int32, (kw, kw), 0)
        pc = lax.broadcasted_iota(jnp.int32, (kw, kw), 1)
        swap = (pr == (pc ^ (HEAD_K // 2))).astype(BF16)

        def rotary(x_bf16, rows):
            xr = _dot(x_bf16, swap)
            return x_bf16.astype(F32) * cos_ref[rows, :] + xr * sin_ref[rows, :]

        lg = _log_sigmoid(logit_ref[...])
        lgf, lgb = lg[0:1], lg[1:2]
        pos = lax.broadcasted_iota(jnp.int32, (c, kw), 0).astype(F32)
        b_const, e_const = (pos + 1.0) * lgf, pos * lgb
        totf_const, totb_const = float(c) * lgf, float(c) * lgb
        rel = (ri - ci).astype(F32)
        dmask = []
        for h in range(PAIR_HEADS):
            lf = lgf[:, h * HEAD_K:h * HEAD_K + 1]
            lb = lgb[:, h * HEAD_K:h * HEAD_K + 1]
            dmask.append(jnp.where(lower, jnp.exp(lf * jnp.maximum(rel, 0.0)), 0.0)
                         + jnp.where(upper, jnp.exp(lb * jnp.maximum(-rel, 0.0)), 0.0))
    else:
        ga = ga_ref[...].astype(BF16)
        b_scr[...] = _log_sigmoid(_dot(ga, wf_ref[...]) + gb_ref[0:1, :]) * (1.0 / GLA_TAU)
        lab_scr[...] = _log_sigmoid(_dot(ga, wb_ref[...]) + gb_ref[1:2, :]) * (1.0 / GLA_TAU)
        row_in_chunk = lax.broadcasted_iota(jnp.int32, (c, kw), 0)

        def chunk_cumsum(x):
            shift = 1
            while shift < c:
                x = x + jnp.where(row_in_chunk >= shift, pltpu.roll(x, shift, 0), 0.0)
                shift *= 2
            return x

    def phase1(i, r_state):
        n = n_chunks - 1 - i
        rows = chunk_rows(n)
        v = v_ref[rows, :]
        if mode == "gla":
            k = k_ref[rows, :].astype(F32)
            lab = lab_scr[rows, :]
            b, einc = chunk_cumsum(b_scr[rows, :]), chunk_cumsum(lab)
            b_scr[rows, :] = b
            einc_scr[rows, :] = einc
            e, totf, totb = einc - lab, b[c - 1:c, :], einc[c - 1:c, :]
        else:
            k = rotary(k_ref[rows, :], rows)
            kx_scr[rows, :] = k
            b, e, totf, totb = b_const, e_const, totf_const, totb_const
        kf = (k * jnp.exp(totf - b)).astype(BF16)
        kb = (k * jnp.exp(e)).astype(BF16)
        kvf = lax.dot_general(v, kf, _TN, preferred_element_type=F32)
        kvb = lax.dot_general(v, kb, _TN, preferred_element_type=F32)
        kvf_scr[n] = jnp.where(same_head, kvf, 0.0)
        rnext_scr[n] = r_state.astype(BF16)
        return jnp.exp(totb) * r_state + jnp.where(same_head, kvb, 0.0)

    unroll = min(n_chunks, LIN_UNROLL)
    lax.fori_loop(0, n_chunks, phase1, jnp.zeros((vw, kw), F32), unroll=unroll)

    def rows_of_chunks(values):
        return jnp.concatenate([jnp.broadcast_to(x, (c, kw)) for x in values], axis=0)

    def phase2(nb, s_state):
        rows = pl.ds(pl.multiple_of(nb * blk, blk), blk)
        v = v_ref[rows, :]
        if mode == "gla":
            q = q_ref[rows, :].astype(F32)
            k = k_ref[rows, :].astype(F32)
            b, einc = b_scr[rows, :], einc_scr[rows, :]
            e = einc - lab_scr[rows, :]
            totf = [b[g * c + c - 1:(g + 1) * c, :] for g in range(grp)]
            totb = [einc[g * c + c - 1:(g + 1) * c, :] for g in range(grp)]
        else:
            q = rotary(q_ref[rows, :], rows)
            k = kx_scr[rows, :]
            b, e, totf, totb = b_const, e_const, [totf_const], [totb_const]

        qf = (q * jnp.exp(b)).astype(BF16)
        qb = (q * jnp.exp(rows_of_chunks(totb) - e)).astype(BF16)
        inter = []
        for g in range(grp):
            n = nb * grp + g
            crows = slice(g * c, (g + 1) * c)
            inter.append(lax.dot_general(qf[crows], s_state.astype(BF16), _NT, preferred_element_type=F32)
                         + lax.dot_general(qb[crows], rnext_scr[n], _NT, preferred_element_type=F32))
            s_state = jnp.exp(totf[g]) * s_state + kvf_scr[n]
        inter = jnp.concatenate(inter, axis=0)

        if mode == "gla":
            bm = rows_of_chunks([b[g * c + c // 2 - 1:g * c + c // 2, :] for g in range(grp)])
            em = rows_of_chunks([e[g * c + c // 2:g * c + c // 2 + 1, :] for g in range(grp)])
            qtf, ktf = q * jnp.exp(b - bm), (k * jnp.exp(bm - b)).astype(BF16)
            qtb, ktb = q * jnp.exp(em - e), (k * jnp.exp(e - em)).astype(BF16)
        else:
            kb16 = k.astype(BF16)
        outs = []
        for h in range(PAIR_HEADS):
            mh = head0_k if h == 0 else jnp.logical_not(head0_k)
            if mode == "gla":
                sf = lax.dot_general(jnp.where(mh, qtf, 0.0).astype(BF16), ktf, _NT, preferred_element_type=F32)
                sb = lax.dot_general(jnp.where(mh, qtb, 0.0).astype(BF16), ktb, _NT, preferred_element_type=F32)
                sc = jnp.where(lower, sf, 0.0) + jnp.where(upper, sb, 0.0)
            else:
                sc = lax.dot_general(jnp.where(mh, q, 0.0).astype(BF16), kb16, _NT,
                                     preferred_element_type=F32) * dmask[h]
            outs.append(_dot(sc.astype(BF16), v[:, h * HEAD_V:(h + 1) * HEAD_V]))

        gate = gate_ref[rows, :].astype(F32)
        for h in range(PAIR_HEADS):
            cols = slice(h * HEAD_V, (h + 1) * HEAD_V)
            o = outs[h] + inter[:, cols]
            o = _rms_rows(o, gain_ref[:, cols]) * _silu(gate[:, cols])
            o_ref[rows, cols] = o.astype(BF16)
        return s_state

    n_blocks = n_chunks // grp
    lax.fori_loop(0, n_blocks, phase2, jnp.zeros((vw, kw), F32), unroll=min(n_blocks, max(1, LIN_UNROLL // grp)))


def _linattn(mode, proj, col0, gain, extra, batch, t):
    chunk = GLA_CHUNK if mode == "gla" else RET_CHUNK
    n_chunks = t // chunk
    kw, vw = PAIR_HEADS * HEAD_K, PAIR_HEADS * HEAD_V
    v0 = (col0 + 2 * N_PAIRS) // 2
    g0 = v0 + N_PAIRS
    in_specs = [
        pl.BlockSpec((t, kw), lambda b, p: (b, col0 + p)),
        pl.BlockSpec((t, kw), lambda b, p: (b, col0 + N_PAIRS + p)),
        pl.BlockSpec((t, vw), lambda b, p: (b, v0 + p)),
        pl.BlockSpec((t, vw), lambda b, p: (b, g0 + p)),
        pl.BlockSpec((1, vw), lambda b, p: (0, p)),
    ]
    args = [proj, proj, proj, proj, gain]
    state_scratch = [pltpu.VMEM((n_chunks, vw, kw), F32), pltpu.VMEM((n_chunks, vw, kw), BF16)]
    if mode == "gla":
        ga, wf, wb, gb = extra
        in_specs += [
            pl.BlockSpec((t, LANES), lambda b, p: (b, 0)),
            pl.BlockSpec((LANES, kw), lambda b, p: (0, p)),
            pl.BlockSpec((LANES, kw), lambda b, p: (0, p)),
            pl.BlockSpec((2, kw), lambda b, p: (0, p)),
        ]
        args += [ga, wf, wb, gb]
        scratch = [pltpu.VMEM((t, kw), F32)] * 3 + state_scratch
    else:
        cos, sin, logit = extra
        in_specs += [
            pl.BlockSpec((t, kw), lambda b, p: (0, 0)),
            pl.BlockSpec((t, kw), lambda b, p: (0, 0)),
            pl.BlockSpec((2, kw), lambda b, p: (0, p)),
        ]
        args += [cos, sin, logit]
        scratch = [pltpu.VMEM((t, kw), F32)] + state_scratch
    return pl.pallas_call(
        functools.partial(_linattn_kernel, mode=mode, chunk=chunk, n_chunks=n_chunks),
        grid=(batch, N_PAIRS),
        in_specs=in_specs,
        out_specs=pl.BlockSpec((t, vw), lambda b, p: (b, p)),
        out_shape=jax.ShapeDtypeStruct((batch * t, N_PAIRS * vw), BF16),
        scratch_shapes=scratch,
        compiler_params=_params(("parallel", "arbitrary")),
        name="linattn_" + mode,
    )(*args)


def _outproj_kernel(x_ref, a_ref, b_ref, c_ref, wa_ref, wb_ref, wc_ref, o_ref):
    o_ref[...] = (x_ref[...] + _dot(a_ref[...], wa_ref[...].astype(BF16))
                  + _dot(b_ref[...], wb_ref[...].astype(BF16)) + _dot(c_ref[...], wc_ref[...].astype(BF16)))


def _outproj(x2, a, b, c, w_o, layer):
    m, d = x2.shape
    tm, tn = min(MM_TILE_M // 2, m), d
    wa_rows, wb_rows = a.shape[1], b.shape[1]
    nb = wa_rows // wb_rows
    return pl.pallas_call(
        _outproj_kernel,
        grid=(m // tm, d // tn),
        in_specs=[
            pl.BlockSpec((tm, tn), lambda i, j: (i, j)),
            pl.BlockSpec((tm, wa_rows), lambda i, j: (i, 0)),
            pl.BlockSpec((tm, wb_rows), lambda i, j: (i, 0)),
            pl.BlockSpec((tm, wb_rows), lambda i, j: (i, 0)),
            pl.BlockSpec((None, wa_rows, tn), lambda i, j: (layer, 0, j), pipeline_mode=pl.Buffered(1)),
            pl.BlockSpec((None, wb_rows, tn), lambda i, j: (layer, nb, j), pipeline_mode=pl.Buffered(1)),
            pl.BlockSpec((None, wb_rows, tn), lambda i, j: (layer, nb + 1, j), pipeline_mode=pl.Buffered(1)),
        ],
        out_specs=pl.BlockSpec((tm, tn), lambda i, j: (i, j)),
        out_shape=jax.ShapeDtypeStruct((m, d), F32),
        compiler_params=_params(("parallel", "arbitrary")),
        name="outproj",
    )(x2, a, b, c, w_o, w_o, w_o)


def _ffn_up_kernel(x_ref, xp_ref, xn_ref, g_ref, wa_ref, wg_ref, cwa_ref, cwg_ref, cba_ref, cbg_ref,
                   o_ref, h_scr, *, tiles_per_seq):
    i = pl.program_id(0)
    tm = x_ref.shape[0]

    @pl.when(pl.program_id(1) == 0)
    def _():
        g = g_ref[...]
        keep_prev = (i % tiles_per_seq != 0).astype(F32)
        keep_next = ((i + 1) % tiles_per_seq != 0).astype(F32)
        h_scr[0:HALO, :] = (_rms_rows(xp_ref[...], g) * keep_prev).astype(BF16)
        h_scr[HALO:HALO + tm, :] = _rms_rows(x_ref[...], g).astype(BF16)
        h_scr[HALO + tm:, :] = (_rms_rows(xn_ref[...], g) * keep_next).astype(BF16)

    h = h_scr[...]
    n_ext = tm + 2 * HALO

    def conv(w_ref, cw_ref, cb_ref):
        u = _dot(h, w_ref[...].astype(BF16))
        prev = pltpu.roll(u, 1, 0)[HALO:HALO + tm]
        nxt = pltpu.roll(u, n_ext - 1, 0)[HALO:HALO + tm]
        return prev * cw_ref[0:1, :] + u[HALO:HALO + tm] * cw_ref[1:2, :] + nxt * cw_ref[2:3, :] + cb_ref[...]

    gate = _silu(conv(wg_ref, cwg_ref, cbg_ref))
    o_ref[...] = (conv(wa_ref, cwa_ref, cba_ref) * gate).astype(BF16)


def _ffn_up(x2, g, w_up, conv_w, conv_b, t, layer):
    m, d = x2.shape
    d_ff = w_up.shape[-1] // 2
    tm = min(MM_TILE_M, t)
    tn = FFN_TILE_N
    nj = d_ff // tn
    blocks_per_tile = tm // HALO
    last_block = m // HALO - 1
    return pl.pallas_call(
        functools.partial(_ffn_up_kernel, tiles_per_seq=t // tm),
        grid=(m // tm, nj),
        in_specs=[
            pl.BlockSpec((tm, d), lambda i, j: (i, 0)),
            pl.BlockSpec((HALO, d), lambda i, j: (jnp.maximum(i * blocks_per_tile - 1, 0), 0)),
            pl.BlockSpec((HALO, d), lambda i, j: (jnp.minimum((i + 1) * blocks_per_tile, last_block), 0)),
            pl.BlockSpec((1, d), lambda i, j: (0, 0)),
            pl.BlockSpec((None, d, tn), lambda i, j: (layer, 0, j)),
            pl.BlockSpec((None, d, tn), lambda i, j: (layer, 0, nj + j)),
            pl.BlockSpec((CONV_W, tn), lambda i, j: (0, j)),
            pl.BlockSpec((CONV_W, tn), lambda i, j: (0, nj + j)),
            pl.BlockSpec((1, tn), lambda i, j: (0, j)),
            pl.BlockSpec((1, tn), lambda i, j: (0, nj + j)),
        ],
        out_specs=pl.BlockSpec((tm, tn), lambda i, j: (i, j)),
        out_shape=jax.ShapeDtypeStruct((m, d_ff), BF16),
        scratch_shapes=[pltpu.VMEM((tm + 2 * HALO, d), BF16)],
        compiler_params=_params(("parallel", "arbitrary")),
        name="ffn_up",
    )(x2, x2, x2, g, w_up, w_up, conv_w, conv_w, conv_b, conv_b)


def _ffn_down_kernel(x_ref, a_ref, w_ref, o_ref):
    o_ref[...] = x_ref[...] + _dot(a_ref[...], w_ref[...])


def _ffn_down(x2, act, w_down, layer):
    m, d = x2.shape
    k = act.shape[1]
    tm, tn = min(MM_TILE_M, m), FFN_TILE_N
    return pl.pallas_call(
        _ffn_down_kernel,
        grid=(m // tm, d // tn),
        in_specs=[
            pl.BlockSpec((tm, tn), lambda i, j: (i, j)),
            pl.BlockSpec((tm, k), lambda i, j: (i, 0)),
            pl.BlockSpec((None, k, tn), lambda i, j: (layer, 0, j)),
        ],
        out_specs=pl.BlockSpec((tm, tn), lambda i, j: (i, j)),
        out_shape=jax.ShapeDtypeStruct((m, d), F32),
        compiler_params=_params(("parallel", "arbitrary")),
        name="ffn_down",
    )(x2, act, w_down)


def _rmsnorm_kernel(x_ref, g_ref, o_ref):
    o_ref[...] = _rms_rows(x_ref[...], g_ref[...])


def _rmsnorm(x2, g):
    m, d = x2.shape
    tm = min(MM_TILE_M, m)
    return pl.pallas_call(
        _rmsnorm_kernel,
        grid=(m // tm,),
        in_specs=[pl.BlockSpec((tm, d), lambda i: (i, 0)), pl.BlockSpec((1, d), lambda i: (0, 0))],
        out_specs=pl.BlockSpec((tm, d), lambda i: (i, 0)),
        out_shape=jax.ShapeDtypeStruct((m, d), F32),
        compiler_params=_params(("parallel",)),
        name="final_rmsnorm",
    )(x2, g)


def _rotary_tables(t):
    half = HEAD_K // 2
    inv = 1.0 / (ROPE_BASE ** jnp.linspace(0.0, 1.0, half, dtype=F32))
    ang = jnp.arange(t, dtype=jnp.int32).astype(F32)[:, None] * inv[None, :]
    cos, sin = jnp.cos(ang), jnp.sin(ang)
    cos_l = jnp.tile(cos, (1, 2 * PAIR_HEADS))
    sin_l = jnp.tile(jnp.concatenate([-sin, sin], axis=-1), (1, PAIR_HEADS))
    return cos_l, sin_l


def kernel(x, ln1_g, w_in, diff_lambda, rel_bias, gla_gate_w, gla_gate_b, ret_decay_logit, head_gain,
           w_o, ln2_g, w_up, conv_w, conv_b, w_down, final_g):
    batch, t, d = x.shape
    depth = w_in.shape[0]
    diff_w = DIFF_HEADS * HEAD_V
    gla_w = 2 * N_PAIRS * HEAD_V
    gla_k = 2 * N_PAIRS * HEAD_K
    ga0 = 3 * diff_w + 2 * gla_k + 2 * gla_w
    ga1 = ga0 + 2 * GLA_GATE_RANK
    gla_col0 = 3 * diff_w // LANES
    ret_col0 = ga0 // LANES

    w_main = jnp.concatenate([w_in[:, :, :ga0], w_in[:, :, ga1:]], axis=-1).astype(BF16)
    w_ga = jnp.pad(w_in[:, :, ga0:ga1], ((0, 0), (0, 0), (0, LANES - 2 * GLA_GATE_RANK))).astype(BF16)
    w_down16 = w_down.astype(BF16)
    k_scale = HEAD_K ** -0.5
    colscale = jnp.ones((1, w_main.shape[-1]), F32)
    colscale = colscale.at[:, :diff_w].set(k_scale * LOG2E)
    colscale = colscale.at[:, 3 * diff_w:3 * diff_w + gla_k].set(k_scale)
    colscale = colscale.at[:, ga0 + gla_k:ga0 + 2 * gla_k].set(k_scale)
    r = GLA_GATE_RANK
    gate_wf = jnp.pad(gla_gate_w[:, 0], ((0, 0), (0, LANES - r), (0, 0))).astype(BF16)
    gate_wb = jnp.pad(gla_gate_w[:, 1], ((0, 0), (r, LANES - 2 * r), (0, 0))).astype(BF16)
    ret_logit = jnp.repeat(ret_decay_logit, HEAD_K, axis=-1)

    rel = (jnp.arange(ATT_TILE, dtype=jnp.int32)[:, None] - jnp.arange(ATT_TILE, dtype=jnp.int32)[None, :])
    rel = rel[None] + ATT_TILE * (jnp.arange(5, dtype=jnp.int32) - 2)[:, None, None]
    bias_tiles = _bias_tiles(rel_bias.astype(F32), _t5_bucket(rel))
    cos_l, sin_l = _rotary_tables(t)

    x2 = x.reshape(batch * t, d)
    for layer in range(depth):
        lam_init = 0.8 - 0.6 * math.exp(-0.3 * layer)
        gain = head_gain[layer][None, :]
        proj, ga = _inproj(x2, ln1_g[layer][None, :], w_main, w_ga, colscale, layer)
        a_out = _diff_attention(proj, bias_tiles, diff_lambda[layer], gain[:, :diff_w], lam_init, batch, t)
        b_out = _linattn("gla", proj, gla_col0, gain[:, diff_w:diff_w + gla_w],
                         (ga, gate_wf[layer], gate_wb[layer], gla_gate_b[layer]), batch, t)
        c_out = _linattn("ret", proj, ret_col0, gain[:, diff_w + gla_w:],
                         (cos_l, sin_l, ret_logit[layer]), batch, t)
        x2 = _outproj(x2, a_out, b_out, c_out, w_o, layer)
        act = _ffn_up(x2, ln2_g[layer][None, :], w_up, conv_w[layer], conv_b[layer][None, :], t, layer)
        x2 = _ffn_down(x2, act, w_down16, layer)
    return _rmsnorm(x2, final_g[None, :]).reshape(batch, t, d)
```

```python
import functools
import math

import jax
import jax.numpy as jnp
from jax import lax
from jax.experimental import pallas as pl
from jax.experimental.pallas import tpu as pltpu

F32 = jnp.float32
BF16 = jnp.bfloat16

HEAD_V = 128
DIFF_HEADS = 8
PAIR_HEADS = 2
N_PAIRS = 2
HEAD_K = 64
GLA_GATE_RANK = 16
GLA_TAU = 16.0
ROPE_BASE = 10000.0
N_BUCKETS = 32
MAX_DISTANCE = 128
CONV_W = 3
EPS = 1e-6
LOG2E = math.log2(math.e)

LANES = 128
SUBLANES = 8
BF16_ROWS = 16
VMEM_LIMIT_BYTES = 56 * 1024 * 1024

ATT_TILE = 256
BIAS_REACH = -(-MAX_DISTANCE // ATT_TILE) + 1
ATT_QTILE = 1024
ATT_KSTAGE = 1024
GLA_CHUNK = 64
RET_CHUNK = 256
LIN_BLOCK_ROWS = 256
LIN_UNROLL = 16
MM_TILE_M = 1024
MM_TILE_N = 1024
FFN_TILE_N = 512
HALO = SUBLANES

NEG_BIG = -1e30

_NT = (((1,), (1,)), ((), ()))
_TN = (((0,), (0,)), ((), ()))


def _params(sem):
    return pltpu.CompilerParams(dimension_semantics=sem, vmem_limit_bytes=VMEM_LIMIT_BYTES)


def _dot(a, b):
    return jnp.dot(a, b, preferred_element_type=F32)


def _rms_rows(x, g):
    return x * lax.rsqrt(jnp.mean(x * x, axis=-1, keepdims=True) + EPS) * g


def _log_sigmoid(z):
    return jnp.minimum(z, 0.0) - jnp.log(1.0 + jnp.exp(-jnp.abs(z)))


def _silu(z):
    return z * (1.0 / (1.0 + jnp.exp(-z)))


def _inproj_kernel(x_ref, g_ref, w_ref, wga_ref, cs_ref, proj_ref, ga_ref, h_scr):
    @pl.when(pl.program_id(1) == 0)
    def _():
        hb = _rms_rows(x_ref[...], g_ref[...]).astype(BF16)
        h_scr[...] = hb
        ga_ref[...] = _dot(hb, wga_ref[...])

    proj_ref[...] = (_dot(h_scr[...], w_ref[...]) * cs_ref[...]).astype(BF16)


def _inproj(x2, g, w, wga, colscale, layer):
    m, d = x2.shape
    n = w.shape[-1]
    tm, tn = min(MM_TILE_M, m), min(MM_TILE_N, n)
    return pl.pallas_call(
        _inproj_kernel,
        grid=(m // tm, n // tn),
        in_specs=[
            pl.BlockSpec((tm, d), lambda i, j: (i, 0)),
            pl.BlockSpec((1, d), lambda i, j: (0, 0)),
            pl.BlockSpec((None, d, tn), lambda i, j: (layer, 0, j)),
            pl.BlockSpec((None, d, LANES), lambda i, j: (layer, 0, 0)),
            pl.BlockSpec((1, tn), lambda i, j: (0, j)),
        ],
        out_specs=[
            pl.BlockSpec((tm, tn), lambda i, j: (i, j)),
            pl.BlockSpec((tm, LANES), lambda i, j: (i, 0)),
        ],
        out_shape=[jax.ShapeDtypeStruct((m, n), BF16), jax.ShapeDtypeStruct((m, LANES), F32)],
        scratch_shapes=[pltpu.VMEM((tm, d), BF16)],
        compiler_params=_params(("parallel", "arbitrary")),
        name="inproj",
    )(x2, g, w, wga, colscale)


def _bias_tiles_kernel(rb_ref, bucket_ref, out_ref):
    h = pl.program_id(0)
    bucket = bucket_ref[...]
    acc = jnp.zeros(bucket.shape, F32)
    for b in range(N_BUCKETS):
        acc = jnp.where(bucket == b, rb_ref[b, h], acc)
    out_ref[...] = acc * LOG2E


def _bias_tiles(rel_bias, bucket):
    nd = bucket.shape[0]
    return pl.pallas_call(
        _bias_tiles_kernel,
        grid=(DIFF_HEADS,),
        in_specs=[
            pl.BlockSpec(memory_space=pltpu.SMEM),
            pl.BlockSpec((nd, ATT_TILE, ATT_TILE), lambda h: (0, 0, 0)),
        ],
        out_specs=pl.BlockSpec((None, nd, ATT_TILE, ATT_TILE), lambda h: (h, 0, 0, 0)),
        out_shape=jax.ShapeDtypeStruct((DIFF_HEADS, nd, ATT_TILE, ATT_TILE), F32),
        compiler_params=_params(("arbitrary",)),
        name="bias_tiles",
    )(rel_bias, bucket)


def _t5_bucket(rel):
    half = N_BUCKETS // 2
    max_exact = half // 2
    ret = jnp.where(rel > 0, half, 0)
    n = jnp.abs(rel)
    nf = jnp.maximum(n, 1).astype(F32)
    large = max_exact + (jnp.log(nf / max_exact) / math.log(MAX_DISTANCE / max_exact)
                         * (half - max_exact)).astype(jnp.int32)
    large = jnp.minimum(large, half - 1)
    return ret + jnp.where(n < max_exact, n, large)


ACC_ROWS = HEAD_V + BF16_ROWS


def _attn_kernel(q_ref, k_ref, v_ref, bias_ref, lam_ref, gain_ref, o_ref,
                 vt_scr, qh_scr, s_scr, p_scr, mloc_scr, alpha_scr, m_scr, acc_scr,
                 *, lam_init, n_stages, stage_keys, q_tile_rows):
    tile = ATT_TILE
    tq, ks = q_tile_rows, stage_keys
    n_qb = tq // tile
    n_kb = ks // tile
    ones_rows = (lax.broadcasted_iota(jnp.int32, (BF16_ROWS, ks), 0) == 0).astype(BF16)
    for c in range(n_stages):
        for kb in range(n_kb):
            vc = v_ref[c * ks + kb * tile:c * ks + (kb + 1) * tile, :].astype(F32)
            vt_scr[c, 0:HEAD_V, kb * tile:(kb + 1) * tile] = vc.T.astype(BF16)
        vt_scr[c, HEAD_V:ACC_ROWS, :] = ones_rows

    lp = lam_ref[...]
    lam = (jnp.exp(jnp.sum(lp[0:1] * lp[1:2], axis=-1, keepdims=True))
           - jnp.exp(jnp.sum(lp[2:3] * lp[3:4], axis=-1, keepdims=True)) + lam_init)
    first_half = lax.broadcasted_iota(jnp.int32, (1, 2 * HEAD_K), 1) < HEAD_K
    gain = gain_ref[...] * (1.0 - lam_init)

    def tile_rows(qi):
        return pl.ds(pl.multiple_of(qi * tq, tq), tq)

    def start_tile(qi):
        q = q_ref[tile_rows(qi), :]
        zero = jnp.zeros_like(q)
        qh_scr[0] = jnp.where(first_half, q, zero)
        qh_scr[1] = jnp.where(first_half, zero, q)
        for half in range(2):
            m_scr[half] = jnp.full((1, tq), NEG_BIG, F32)
            acc_scr[half, :, 0:tq] = jnp.zeros((ACC_ROWS, tq), F32)
        stage_a(qi, 0)

    def stage_a(qi, c):
        kc = k_ref[pl.ds(pl.multiple_of(c * ks, ks), ks), :]
        for qb in range(n_qb):
            cols = slice(qb * tile, (qb + 1) * tile)
            s = [lax.dot_general(kc, qh_scr[half, cols, :], _NT, preferred_element_type=F32)
                 for half in range(2)]
            mloc = [None, None]
            for kb in range(n_kb):
                krows = slice(kb * tile, (kb + 1) * tile)
                offset = (c * n_kb + kb) - (qi * n_qb + qb)
                bias = bias_ref[jnp.clip(offset, -BIAS_REACH, BIAS_REACH) + BIAS_REACH]
                for half in range(2):
                    sb = s[half][krows] + bias
                    s_scr[half, krows, cols] = sb
                    bmax = jnp.max(sb, axis=0, keepdims=True)
                    mloc[half] = bmax if kb == 0 else jnp.maximum(mloc[half], bmax)
            for half in range(2):
                mloc_scr[half, :, cols] = mloc[half]

    def stage_b():
        for half in range(2):
            m_old = m_scr[half]
            m_new = jnp.maximum(m_old, mloc_scr[half])
            alpha_scr[half] = jnp.exp2(m_old - m_new)
            m_scr[half] = m_new
            p_scr[half, :, 0:tq] = jnp.exp2(s_scr[half, :, 0:tq] - m_new).astype(BF16)

    def stage_c(c):
        vt = vt_scr[c]
        for half in range(2):
            acc_scr[half, :, 0:tq] = (acc_scr[half, :, 0:tq] * alpha_scr[half]
                                      + _dot(vt, p_scr[half, :, 0:tq]))

    def finish_tile(qi):
        acc1, acc2 = acc_scr[0, :, 0:tq], acc_scr[1, :, 0:tq]
        o1 = acc1[0:HEAD_V] / acc1[HEAD_V:HEAD_V + 1]
        o2 = acc2[0:HEAD_V] / acc2[HEAD_V:HEAD_V + 1]
        ot = o1 - lam * o2
        ot = ot * lax.rsqrt(jnp.mean(ot * ot, axis=0, keepdims=True) + EPS)
        o_ref[tile_rows(qi), :] = (ot.T * gain).astype(BF16)

    def q_tile(qi, has_next):
        stage_b()
        stage_a(qi, 1)

        def body(c, _):
            stage_c(c)
            stage_b()
            stage_a(qi, c + 2)
            return 0

        lax.fori_loop(0, n_stages - 2, body, 0)
        stage_c(n_stages - 2)
        stage_b()
        stage_c(n_stages - 1)
        finish_tile(qi)
        if has_next:
            start_tile(qi + 1)

    n_q = (n_stages * ks) // tq
    start_tile(0)

    def q_loop(qi, _):
        q_tile(qi, True)
        return 0

    lax.fori_loop(0, n_q - 1, q_loop, 0)
    q_tile(n_q - 1, False)


def _diff_attention(proj, bias_tiles, lam_params, gain, lam_init, batch, t):
    nd = bias_tiles.shape[1]
    tile = ATT_TILE
    tq = min(ATT_QTILE, t)
    ks = min(ATT_KSTAGE, t // 2)
    n_stages = t // ks
    return pl.pallas_call(
        functools.partial(_attn_kernel, lam_init=lam_init, n_stages=n_stages, stage_keys=ks, q_tile_rows=tq),
        grid=(batch, DIFF_HEADS),
        in_specs=[
            pl.BlockSpec((t, HEAD_V), lambda b, h: (b, h)),
            pl.BlockSpec((t, HEAD_V), lambda b, h: (b, DIFF_HEADS + h)),
            pl.BlockSpec((t, HEAD_V), lambda b, h: (b, 2 * DIFF_HEADS + h)),
            pl.BlockSpec((None, nd, tile, tile), lambda b, h: (h, 0, 0, 0)),
            pl.BlockSpec(lam_params.shape, lambda b, h: (0, 0)),
            pl.BlockSpec((1, HEAD_V), lambda b, h: (0, h)),
        ],
        out_specs=pl.BlockSpec((t, HEAD_V), lambda b, h: (b, h)),
        out_shape=jax.ShapeDtypeStruct((batch * t, DIFF_HEADS * HEAD_V), BF16),
        scratch_shapes=[
            pltpu.VMEM((n_stages, ACC_ROWS, ks), BF16),
            pltpu.VMEM((2, tq, 2 * HEAD_K), BF16),
            pltpu.VMEM((2, ks, tq + LANES), F32),
            pltpu.VMEM((2, ks, tq + LANES), BF16),
            pltpu.VMEM((2, 1, tq), F32),
            pltpu.VMEM((2, 1, tq), F32),
            pltpu.VMEM((2, 1, tq), F32),
            pltpu.VMEM((2, ACC_ROWS, tq + LANES), F32),
        ],
        compiler_params=_params(("parallel", "arbitrary")),
        name="diff_attn",
    )(proj, proj, proj, bias_tiles, lam_params, gain)


def _linattn_kernel(*refs, mode, chunk, n_chunks):
    if mode == "gla":
        (q_ref, k_ref, v_ref, gate_ref, gain_ref, ga_ref, wf_ref, wb_ref, gb_ref,
         o_ref, b_scr, einc_scr, lab_scr, kvf_scr, rnext_scr) = refs
    else:
        (q_ref, k_ref, v_ref, gate_ref, gain_ref, cos_ref, sin_ref, logit_ref,
         o_ref, kx_scr, kvf_scr, rnext_scr) = refs
    c = chunk
    kw = PAIR_HEADS * HEAD_K
    vw = PAIR_HEADS * HEAD_V

    lane_k = lax.broadcasted_iota(jnp.int32, (1, kw), 1)
    head0_k = lane_k < HEAD_K
    st_row = lax.broadcasted_iota(jnp.int32, (vw, kw), 0)
    st_lane = lax.broadcasted_iota(jnp.int32, (vw, kw), 1)
    same_head = (st_row < HEAD_V) == (st_lane < HEAD_K)
    grp = max(1, LIN_BLOCK_ROWS // c)
    blk = grp * c
    ri = lax.broadcasted_iota(jnp.int32, (blk, blk), 0)
    ci = lax.broadcasted_iota(jnp.int32, (blk, blk), 1)
    same_chunk = (ri // c) == (ci // c)
    lower, upper = same_chunk & (ri >= ci), same_chunk & (ri <= ci)

    def chunk_rows(n):
        return pl.ds(pl.multiple_of(n * c, c), c)

    if mode == "ret":
        pr = lax.broadcasted_iota(jnp.int32, (kw, kw), 0)
        pc = lax.broadcasted_iota(jnp.int32, (kw, kw), 1)
        swap = (pr == (pc ^ (HEAD_K // 2))).astype(BF16)

        def rotary(x_bf16, rows):
            xr = _dot(x_bf16, swap)
            return x_bf16.astype(F32) * cos_ref[rows, :] + xr * sin_ref[rows, :]

        lg = _log_sigmoid(logit_ref[...])
        lgf, lgb = lg[0:1], lg[1:2]
        pos = lax.broadcasted_iota(jnp.int32, (c, kw), 0).astype(F32)
        b_const, e_const = (pos + 1.0) * lgf, pos * lgb
        totf_const, totb_const = float(c) * lgf, float(c) * lgb
        rel = (ri - ci).astype(F32)
        dmask = []
        for h in range(PAIR_HEADS):
            lf = lgf[:, h * HEAD_K:h * HEAD_K + 1]
            lb = lgb[:, h * HEAD_K:h * HEAD_K + 1]
            dmask.append(jnp.where(lower, jnp.exp(lf * jnp.maximum(rel, 0.0)), 0.0)
                         + jnp.where(upper, jnp.exp(lb * jnp.maximum(-rel, 0.0)), 0.0))
    else:
        ga = ga_ref[...].astype(BF16)
        b_scr[...] = _log_sigmoid(_dot(ga, wf_ref[...]) + gb_ref[0:1, :]) * (1.0 / GLA_TAU)
        lab_scr[...] = _log_sigmoid(_dot(ga, wb_ref[...]) + gb_ref[1:2, :]) * (1.0 / GLA_TAU)
        row_in_chunk = lax.broadcasted_iota(jnp.int32, (c, kw), 0)

        def chunk_cumsum(x):
            shift = 1
            while shift < c:
                x = x + jnp.where(row_in_chunk >= shift, pltpu.roll(x, shift, 0), 0.0)
                shift *= 2
            return x

    def phase1(i, r_state):
        n = n_chunks - 1 - i
        rows = chunk_rows(n)
        v = v_ref[rows, :]
        if mode == "gla":
            k = k_ref[rows, :].astype(F32)
            lab = lab_scr[rows, :]
            b, einc = chunk_cumsum(b_scr[rows, :]), chunk_cumsum(lab)
            b_scr[rows, :] = b
            einc_scr[rows, :] = einc
            e, totf, totb = einc - lab, b[c - 1:c, :], einc[c - 1:c, :]
        else:
            k = rotary(k_ref[rows, :], rows)
            kx_scr[rows, :] = k
            b, e, totf, totb = b_const, e_const, totf_const, totb_const
        kf = (k * jnp.exp(totf - b)).astype(BF16)
        kb = (k * jnp.exp(e)).astype(BF16)
        kvf = lax.dot_general(v, kf, _TN, preferred_element_type=F32)
        kvb = lax.dot_general(v, kb, _TN, preferred_element_type=F32)
        kvf_scr[n] = jnp.where(same_head, kvf, 0.0)
        rnext_scr[n] = r_state.astype(BF16)
        return jnp.exp(totb) * r_state + jnp.where(same_head, kvb, 0.0)

    unroll = min(n_chunks, LIN_UNROLL)
    lax.fori_loop(0, n_chunks, phase1, jnp.zeros((vw, kw), F32), unroll=unroll)

    def rows_of_chunks(values):
        return jnp.concatenate([jnp.broadcast_to(x, (c, kw)) for x in values], axis=0)

    def phase2(nb, s_state):
        rows = pl.ds(pl.multiple_of(nb * blk, blk), blk)
        v = v_ref[rows, :]
        if mode == "gla":
            q = q_ref[rows, :].astype(F32)
            k = k_ref[rows, :].astype(F32)
            b, einc = b_scr[rows, :], einc_scr[rows, :]
            e = einc - lab_scr[rows, :]
            totf = [b[g * c + c - 1:(g + 1) * c, :] for g in range(grp)]
            totb = [einc[g * c + c - 1:(g + 1) * c, :] for g in range(grp)]
        else:
            q = rotary(q_ref[rows, :], rows)
            k = kx_scr[rows, :]
            b, e, totf, totb = b_const, e_const, [totf_const], [totb_const]

        qf = (q * jnp.exp(b)).astype(BF16)
        qb = (q * jnp.exp(rows_of_chunks(totb) - e)).astype(BF16)
        inter = []
        for g in range(grp):
            n = nb * grp + g
            crows = slice(g * c, (g + 1) * c)
            inter.append(lax.dot_general(qf[crows], s_state.astype(BF16), _NT, preferred_element_type=F32)
                         + lax.dot_general(qb[crows], rnext_scr[n], _NT, preferred_element_type=F32))
            s_state = jnp.exp(totf[g]) * s_state + kvf_scr[n]
        inter = jnp.concatenate(inter, axis=0)

        if mode == "gla":
            bm = rows_of_chunks([b[g * c + c // 2 - 1:g * c + c // 2, :] for g in range(grp)])
            em = rows_of_chunks([e[g * c + c // 2:g * c + c // 2 + 1, :] for g in range(grp)])
            qtf, ktf = q * jnp.exp(b - bm), (k * jnp.exp(bm - b)).astype(BF16)
            qtb, ktb = q * jnp.exp(em - e), (k * jnp.exp(e - em)).astype(BF16)
        else:
            kb16 = k.astype(BF16)
        outs = []
        for h in range(PAIR_HEADS):
            mh = head0_k if h == 0 else jnp.logical_not(head0_k)
            if mode == "gla":
                sf = lax.dot_general(jnp.where(mh, qtf, 0.0).astype(BF16), ktf, _NT, preferred_element_type=F32)
                sb = lax.dot_general(jnp.where(mh, qtb, 0.0).astype(BF16), ktb, _NT, preferred_element_type=F32)
                sc = jnp.where(lower, sf, 0.0) + jnp.where(upper, sb, 0.0)
            else:
                sc = lax.dot_general(jnp.where(mh, q, 0.0).astype(BF16), kb16, _NT,
                                     preferred_element_type=F32) * dmask[h]
            outs.append(_dot(sc.astype(BF16), v[:, h * HEAD_V:(h + 1) * HEAD_V]))

        gate = gate_ref[rows, :].astype(F32)
        for h in range(PAIR_HEADS):
            cols = slice(h * HEAD_V, (h + 1) * HEAD_V)
            o = outs[h] + inter[:, cols]
            o = _rms_rows(o, gain_ref[:, cols]) * _silu(gate[:, cols])
            o_ref[rows, cols] = o.astype(BF16)
        return s_state

    n_blocks = n_chunks // grp
    lax.fori_loop(0, n_blocks, phase2, jnp.zeros((vw, kw), F32), unroll=min(n_blocks, max(1, LIN_UNROLL // grp)))


def _linattn(mode, proj, col0, gain, extra, batch, t):
    chunk = GLA_CHUNK if mode == "gla" else RET_CHUNK
    n_chunks = t // chunk
    kw, vw = PAIR_HEADS * HEAD_K, PAIR_HEADS * HEAD_V
    v0 = (col0 + 2 * N_PAIRS) // 2
    g0 = v0 + N_PAIRS
    in_specs = [
        pl.BlockSpec((t, kw), lambda b, p: (b, col0 + p)),
        pl.BlockSpec((t, kw), lambda b, p: (b, col0 + N_PAIRS + p)),
        pl.BlockSpec((t, vw), lambda b, p: (b, v0 + p)),
        pl.BlockSpec((t, vw), lambda b, p: (b, g0 + p)),
        pl.BlockSpec((1, vw), lambda b, p: (0, p)),
    ]
    args = [proj, proj, proj, proj, gain]
    state_scratch = [pltpu.VMEM((n_chunks, vw, kw), F32), pltpu.VMEM((n_chunks, vw, kw), BF16)]
    if mode == "gla":
        ga, wf, wb, gb = extra
        in_specs += [
            pl.BlockSpec((t, LANES), lambda b, p: (b, 0)),
            pl.BlockSpec((LANES, kw), lambda b, p: (0, p)),
            pl.BlockSpec((LANES, kw), lambda b, p: (0, p)),
            pl.BlockSpec((2, kw), lambda b, p: (0, p)),
        ]
        args += [ga, wf, wb, gb]
        scratch = [pltpu.VMEM((t, kw), F32)] * 3 + state_scratch
    else:
        cos, sin, logit = extra
        in_specs += [
            pl.BlockSpec((t, kw), lambda b, p: (0, 0)),
            pl.BlockSpec((t, kw), lambda b, p: (0, 0)),
            pl.BlockSpec((2, kw), lambda b, p: (0, p)),
        ]
        args += [cos, sin, logit]
        scratch = [pltpu.VMEM((t, kw), F32)] + state_scratch
    return pl.pallas_call(
        functools.partial(_linattn_kernel, mode=mode, chunk=chunk, n_chunks=n_chunks),
        grid=(batch, N_PAIRS),
        in_specs=in_specs,
        out_specs=pl.BlockSpec((t, vw), lambda b, p: (b, p)),
        out_shape=jax.ShapeDtypeStruct((batch * t, N_PAIRS * vw), BF16),
        scratch_shapes=scratch,
        compiler_params=_params(("parallel", "arbitrary")),
        name="linattn_" + mode,
    )(*args)


def _outproj_kernel(x_ref, a_ref, b_ref, c_ref, wa_ref, wb_ref, wc_ref, o_ref):
    o_ref[...] = (x_ref[...] + _dot(a_ref[...], wa_ref[...].astype(BF16))
                  + _dot(b_ref[...], wb_ref[...].astype(BF16)) + _dot(c_ref[...], wc_ref[...].astype(BF16)))


def _outproj(x2, a, b, c, w_o, layer):
    m, d = x2.shape
    tm, tn = min(MM_TILE_M // 2, m), d
    wa_rows, wb_rows = a.shape[1], b.shape[1]
    nb = wa_rows // wb_rows
    return pl.pallas_call(
        _outproj_kernel,
        grid=(m // tm, d // tn),
        in_specs=[
            pl.BlockSpec((tm, tn), lambda i, j: (i, j)),
            pl.BlockSpec((tm, wa_rows), lambda i, j: (i, 0)),
            pl.BlockSpec((tm, wb_rows), lambda i, j: (i, 0)),
            pl.BlockSpec((tm, wb_rows), lambda i, j: (i, 0)),
            pl.BlockSpec((None, wa_rows, tn), lambda i, j: (layer, 0, j), pipeline_mode=pl.Buffered(1)),
            pl.BlockSpec((None, wb_rows, tn), lambda i, j: (layer, nb, j), pipeline_mode=pl.Buffered(1)),
            pl.BlockSpec((None, wb_rows, tn), lambda i, j: (layer, nb + 1, j), pipeline_mode=pl.Buffered(1)),
        ],
        out_specs=pl.BlockSpec((tm, tn), lambda i, j: (i, j)),
        out_shape=jax.ShapeDtypeStruct((m, d), F32),
        compiler_params=_params(("parallel", "arbitrary")),
        name="outproj",
    )(x2, a, b, c, w_o, w_o, w_o)


def _ffn_up_kernel(x_ref, xp_ref, xn_ref, g_ref, wa_ref, wg_ref, cwa_ref, cwg_ref, cba_ref, cbg_ref,
                   o_ref, h_scr, *, tiles_per_seq):
    i = pl.program_id(0)
    tm = x_ref.shape[0]

    @pl.when(pl.program_id(1) == 0)
    def _():
        g = g_ref[...]
        keep_prev = (i % tiles_per_seq != 0).astype(F32)
        keep_next = ((i + 1) % tiles_per_seq != 0).astype(F32)
        h_scr[0:HALO, :] = (_rms_rows(xp_ref[...], g) * keep_prev).astype(BF16)
        h_scr[HALO:HALO + tm, :] = _rms_rows(x_ref[...], g).astype(BF16)
        h_scr[HALO + tm:, :] = (_rms_rows(xn_ref[...], g) * keep_next).astype(BF16)

    h = h_scr[...]
    n_ext = tm + 2 * HALO

    def conv(w_ref, cw_ref, cb_ref):
        u = _dot(h, w_ref[...].astype(BF16))
        prev = pltpu.roll(u, 1, 0)[HALO:HALO + tm]
        nxt = pltpu.roll(u, n_ext - 1, 0)[HALO:HALO + tm]
        return prev * cw_ref[0:1, :] + u[HALO:HALO + tm] * cw_ref[1:2, :] + nxt * cw_ref[2:3, :] + cb_ref[...]

    gate = _silu(conv(wg_ref, cwg_ref, cbg_ref))
    o_ref[...] = (conv(wa_ref, cwa_ref, cba_ref) * gate).astype(BF16)


def _ffn_up(x2, g, w_up, conv_w, conv_b, t, layer):
    m, d = x2.shape
    d_ff = w_up.shape[-1] // 2
    tm = min(MM_TILE_M, t)
    tn = FFN_TILE_N
    nj = d_ff // tn
    blocks_per_tile = tm // HALO
    last_block = m // HALO - 1
    return pl.pallas_call(
        functools.partial(_ffn_up_kernel, tiles_per_seq=t // tm),
        grid=(m // tm, nj),
        in_specs=[
            pl.BlockSpec((tm, d), lambda i, j: (i, 0)),
            pl.BlockSpec((HALO, d), lambda i, j: (jnp.maximum(i * blocks_per_tile - 1, 0), 0)),
            pl.BlockSpec((HALO, d), lambda i, j: (jnp.minimum((i + 1) * blocks_per_tile, last_block), 0)),
            pl.BlockSpec((1, d), lambda i, j: (0, 0)),
            pl.BlockSpec((None, d, tn), lambda i, j: (layer, 0, j)),
            pl.BlockSpec((None, d, tn), lambda i, j: (layer, 0, nj + j)),
            pl.BlockSpec((CONV_W, tn), lambda i, j: (0, j)),
            pl.BlockSpec((CONV_W, tn), lambda i, j: (0, nj + j)),
            pl.BlockSpec((1, tn), lambda i, j: (0, j)),
            pl.BlockSpec((1, tn), lambda i, j: (0, nj + j)),
        ],
        out_specs=pl.BlockSpec((tm, tn), lambda i, j: (i, j)),
        out_shape=jax.ShapeDtypeStruct((m, d_ff), BF16),
        scratch_shapes=[pltpu.VMEM((tm + 2 * HALO, d), BF16)],
        compiler_params=_params(("parallel", "arbitrary")),
        name="ffn_up",
    )(x2, x2, x2, g, w_up, w_up, conv_w, conv_w, conv_b, conv_b)


def _ffn_down_kernel(x_ref, a_ref, w_ref, o_ref):
    o_ref[...] = x_ref[...] + _dot(a_ref[...], w_ref[...])


def _ffn_down(x2, act, w_down, layer):
    m, d = x2.shape
    k = act.shape[1]
    tm, tn = min(MM_TILE_M, m), FFN_TILE_N
    return pl.pallas_call(
        _ffn_down_kernel,
        grid=(m // tm, d // tn),
        in_specs=[
            pl.BlockSpec((tm, tn), lambda i, j: (i, j)),
            pl.BlockSpec((tm, k), lambda i, j: (i, 0)),
            pl.BlockSpec((None, k, tn), lambda i, j: (layer, 0, j)),
        ],
        out_specs=pl.BlockSpec((tm, tn), lambda i, j: (i, j)),
        out_shape=jax.ShapeDtypeStruct((m, d), F32),
        compiler_params=_params(("parallel", "arbitrary")),
        name="ffn_down",
    )(x2, act, w_down)


def _rmsnorm_kernel(x_ref, g_ref, o_ref):
    o_ref[...] = _rms_rows(x_ref[...], g_ref[...])


def _rmsnorm(x2, g):
    m, d = x2.shape
    tm = min(MM_TILE_M, m)
    return pl.pallas_call(
        _rmsnorm_kernel,
        grid=(m // tm,),
        in_specs=[pl.BlockSpec((tm, d), lambda i: (i, 0)), pl.BlockSpec((1, d), lambda i: (0, 0))],
        out_specs=pl.BlockSpec((tm, d), lambda i: (i, 0)),
        out_shape=jax.ShapeDtypeStruct((m, d), F32),
        compiler_params=_params(("parallel",)),
        name="final_rmsnorm",
    )(x2, g)


def _rotary_tables(t):
    half = HEAD_K // 2
    inv = 1.0 / (ROPE_BASE ** jnp.linspace(0.0, 1.0, half, dtype=F32))
    ang = jnp.arange(t, dtype=jnp.int32).astype(F32)[:, None] * inv[None, :]
    cos, sin = jnp.cos(ang), jnp.sin(ang)
    cos_l = jnp.tile(cos, (1, 2 * PAIR_HEADS))
    sin_l = jnp.tile(jnp.concatenate([-sin, sin], axis=-1), (1, PAIR_HEADS))
    return cos_l, sin_l


def kernel(x, ln1_g, w_in, diff_lambda, rel_bias, gla_gate_w, gla_gate_b, ret_decay_logit, head_gain,
           w_o, ln2_g, w_up, conv_w, conv_b, w_down, final_g):
    batch, t, d = x.shape
    depth = w_in.shape[0]
    diff_w = DIFF_HEADS * HEAD_V
    gla_w = 2 * N_PAIRS * HEAD_V
    gla_k = 2 * N_PAIRS * HEAD_K
    ga0 = 3 * diff_w + 2 * gla_k + 2 * gla_w
    ga1 = ga0 + 2 * GLA_GATE_RANK
    gla_col0 = 3 * diff_w // LANES
    ret_col0 = ga0 // LANES

    w_main = jnp.concatenate([w_in[:, :, :ga0], w_in[:, :, ga1:]], axis=-1).astype(BF16)
    w_ga = jnp.pad(w_in[:, :, ga0:ga1], ((0, 0), (0, 0), (0, LANES - 2 * GLA_GATE_RANK))).astype(BF16)
    w_down16 = w_down.astype(BF16)
    k_scale = HEAD_K ** -0.5
    colscale = jnp.ones((1, w_main.shape[-1]), F32)
    colscale = colscale.at[:, :diff_w].set(k_scale * LOG2E)
    colscale = colscale.at[:, 3 * diff_w:3 * diff_w + gla_k].set(k_scale)
    colscale = colscale.at[:, ga0 + gla_k:ga0 + 2 * gla_k].set(k_scale)
    r = GLA_GATE_RANK
    gate_wf = jnp.pad(gla_gate_w[:, 0], ((0, 0), (0, LANES - r), (0, 0))).astype(BF16)
    gate_wb = jnp.pad(gla_gate_w[:, 1], ((0, 0), (r, LANES - 2 * r), (0, 0))).astype(BF16)
    ret_logit = jnp.repeat(ret_decay_logit, HEAD_K, axis=-1)

    rel = (jnp.arange(ATT_TILE, dtype=jnp.int32)[:, None] - jnp.arange(ATT_TILE, dtype=jnp.int32)[None, :])
    tile_offsets = jnp.arange(2 * BIAS_REACH + 1, dtype=jnp.int32) - BIAS_REACH
    rel = rel[None] + ATT_TILE * tile_offsets[:, None, None]
    bias_tiles = _bias_tiles(rel_bias.astype(F32), _t5_bucket(rel))
    cos_l, sin_l = _rotary_tables(t)

    x2 = x.reshape(batch * t, d)
    for layer in range(depth):
        lam_init = 0.8 - 0.6 * math.exp(-0.3 * layer)
        gain = head_gain[layer][None, :]
        proj, ga = _inproj(x2, ln1_g[layer][None, :], w_main, w_ga, colscale, layer)
        a_out = _diff_attention(proj, bias_tiles, diff_lambda[layer], gain[:, :diff_w], lam_init, batch, t)
        b_out = _linattn("gla", proj, gla_col0, gain[:, diff_w:diff_w + gla_w],
                         (ga, gate_wf[layer], gate_wb[layer], gla_gate_b[layer]), batch, t)
        c_out = _linattn("ret", proj, ret_col0, gain[:, diff_w + gla_w:],
                         (cos_l, sin_l, ret_logit[layer]), batch, t)
        x2 = _outproj(x2, a_out, b_out, c_out, w_o, layer)
        act = _ffn_up(x2, ln2_g[layer][None, :], w_up, conv_w[layer], conv_b[layer][None, :], t, layer)
        x2 = _ffn_down(x2, act, w_down16, layer)
    return _rmsnorm(x2, final_g[None, :]).reshape(batch, t, d)
```

```python
import functools
import math

import jax
import jax.numpy as jnp
from jax import lax
from jax.experimental import pallas as pl
from jax.experimental.pallas import tpu as pltpu

F32 = jnp.float32
BF16 = jnp.bfloat16

HEAD_V = 128
DIFF_HEADS = 8
PAIR_HEADS = 2
N_PAIRS = 2
HEAD_K = 64
GLA_GATE_RANK = 16
GLA_TAU = 16.0
ROPE_BASE = 10000.0
N_BUCKETS = 32
MAX_DISTANCE = 128
CONV_W = 3
EPS = 1e-6
LOG2E = math.log2(math.e)

LANES = 128
SUBLANES = 8
BF16_ROWS = 16
VMEM_LIMIT_BYTES = 56 * 1024 * 1024

ATT_TILE = 256
BIAS_REACH = -(-MAX_DISTANCE // ATT_TILE) + 1
ATT_QTILE = 1024
ATT_KSTAGE = 1024
GLA_CHUNK = 64
RET_CHUNK = 256
LIN_BLOCK_ROWS = 256
LIN_UNROLL = 16
MM_TILE_M = 1024
MM_TILE_N = 1024
FFN_TILE_N = 512
FFN_DOWN_TILE_M = 256
HALO = SUBLANES

NEG_BIG = -1e30

_NT = (((1,), (1,)), ((), ()))
_TN = (((0,), (0,)), ((), ()))


def _params(sem):
    return pltpu.CompilerParams(dimension_semantics=sem, vmem_limit_bytes=VMEM_LIMIT_BYTES)


def _dot(a, b):
    return jnp.dot(a, b, preferred_element_type=F32)


def _rms_rows(x, g):
    return x * lax.rsqrt(jnp.mean(x * x, axis=-1, keepdims=True) + EPS) * g


def _log_sigmoid(z):
    return jnp.minimum(z, 0.0) - jnp.log(1.0 + jnp.exp(-jnp.abs(z)))


def _silu(z):
    return z * (1.0 / (1.0 + jnp.exp(-z)))


def _inproj_kernel(x_ref, g_ref, w_ref, wga_ref, cs_ref, proj_ref, ga_ref, h_scr):
    @pl.when(pl.program_id(1) == 0)
    def _():
        hb = _rms_rows(x_ref[...], g_ref[...]).astype(BF16)
        h_scr[...] = hb
        ga_ref[...] = _dot(hb, wga_ref[...])

    proj_ref[...] = (_dot(h_scr[...], w_ref[...]) * cs_ref[...]).astype(BF16)


def _inproj(x2, g, w, wga, colscale, layer):
    m, d = x2.shape
    n = w.shape[-1]
    tm, tn = min(MM_TILE_M, m), min(MM_TILE_N, n)
    return pl.pallas_call(
        _inproj_kernel,
        grid=(m // tm, n // tn),
        in_specs=[
            pl.BlockSpec((tm, d), lambda i, j: (i, 0)),
            pl.BlockSpec((1, d), lambda i, j: (0, 0)),
            pl.BlockSpec((None, d, tn), lambda i, j: (layer, 0, j)),
            pl.BlockSpec((None, d, LANES), lambda i, j: (layer, 0, 0)),
            pl.BlockSpec((1, tn), lambda i, j: (0, j)),
        ],
        out_specs=[
            pl.BlockSpec((tm, tn), lambda i, j: (i, j)),
            pl.BlockSpec((tm, LANES), lambda i, j: (i, 0)),
        ],
        out_shape=[jax.ShapeDtypeStruct((m, n), BF16), jax.ShapeDtypeStruct((m, LANES), F32)],
        scratch_shapes=[pltpu.VMEM((tm, d), BF16)],
        compiler_params=_params(("parallel", "arbitrary")),
        name="inproj",
    )(x2, g, w, wga, colscale)


def _bias_tiles_kernel(rb_ref, bucket_ref, out_ref):
    h = pl.program_id(0)
    bucket = bucket_ref[...]
    acc = jnp.zeros(bucket.shape, F32)
    for b in range(N_BUCKETS):
        acc = jnp.where(bucket == b, rb_ref[b, h], acc)
    out_ref[...] = acc * LOG2E


def _bias_tiles(rel_bias, bucket):
    nd = bucket.shape[0]
    return pl.pallas_call(
        _bias_tiles_kernel,
        grid=(DIFF_HEADS,),
        in_specs=[
            pl.BlockSpec(memory_space=pltpu.SMEM),
            pl.BlockSpec((nd, ATT_TILE, ATT_TILE), lambda h: (0, 0, 0)),
        ],
        out_specs=pl.BlockSpec((None, nd, ATT_TILE, ATT_TILE), lambda h: (h, 0, 0, 0)),
        out_shape=jax.ShapeDtypeStruct((DIFF_HEADS, nd, ATT_TILE, ATT_TILE), F32),
        compiler_params=_params(("arbitrary",)),
        name="bias_tiles",
    )(rel_bias, bucket)


def _t5_bucket(rel):
    half = N_BUCKETS // 2
    max_exact = half // 2
    ret = jnp.where(rel > 0, half, 0)
    n = jnp.abs(rel)
    nf = jnp.maximum(n, 1).astype(F32)
    large = max_exact + (jnp.log(nf / max_exact) / math.log(MAX_DISTANCE / max_exact)
                         * (half - max_exact)).astype(jnp.int32)
    large = jnp.minimum(large, half - 1)
    return ret + jnp.where(n < max_exact, n, large)


ACC_ROWS = HEAD_V + BF16_ROWS


def _attn_kernel(q_ref, k_ref, v_ref, bias_ref, lam_ref, gain_ref, o_ref,
                 vt_scr, qh_scr, s_scr, p_scr, mloc_scr, alpha_scr, m_scr, acc_scr,
                 *, lam_init, n_stages, stage_keys, q_tile_rows):
    tile = ATT_TILE
    tq, ks = q_tile_rows, stage_keys
    n_qb = tq // tile
    n_kb = ks // tile
    ones_rows = (lax.broadcasted_iota(jnp.int32, (BF16_ROWS, ks), 0) == 0).astype(BF16)
    for c in range(n_stages):
        for kb in range(n_kb):
            vc = v_ref[c * ks + kb * tile:c * ks + (kb + 1) * tile, :].astype(F32)
            vt_scr[c, 0:HEAD_V, kb * tile:(kb + 1) * tile] = vc.T.astype(BF16)
        vt_scr[c, HEAD_V:ACC_ROWS, :] = ones_rows

    lp = lam_ref[...]
    lam = (jnp.exp(jnp.sum(lp[0:1] * lp[1:2], axis=-1, keepdims=True))
           - jnp.exp(jnp.sum(lp[2:3] * lp[3:4], axis=-1, keepdims=True)) + lam_init)
    first_half = lax.broadcasted_iota(jnp.int32, (1, 2 * HEAD_K), 1) < HEAD_K
    gain = gain_ref[...] * (1.0 - lam_init)

    def tile_rows(qi):
        return pl.ds(pl.multiple_of(qi * tq, tq), tq)

    def start_tile(qi):
        q = q_ref[tile_rows(qi), :]
        zero = jnp.zeros_like(q)
        qh_scr[0] = jnp.where(first_half, q, zero)
        qh_scr[1] = jnp.where(first_half, zero, q)
        for half in range(2):
            m_scr[half] = jnp.full((1, tq), NEG_BIG, F32)
            acc_scr[half, :, 0:tq] = jnp.zeros((ACC_ROWS, tq), F32)
        stage_a(qi, 0)

    def stage_a(qi, c):
        kc = k_ref[pl.ds(pl.multiple_of(c * ks, ks), ks), :]
        for qb in range(n_qb):
            cols = slice(qb * tile, (qb + 1) * tile)
            s = [lax.dot_general(kc, qh_scr[half, cols, :], _NT, preferred_element_type=F32)
                 for half in range(2)]
            mloc = [None, None]
            for kb in range(n_kb):
                krows = slice(kb * tile, (kb + 1) * tile)
                offset = (c * n_kb + kb) - (qi * n_qb + qb)
                bias = bias_ref[jnp.clip(offset, -BIAS_REACH, BIAS_REACH) + BIAS_REACH]
                for half in range(2):
                    sb = s[half][krows] + bias
                    s_scr[half, krows, cols] = sb
                    bmax = jnp.max(sb, axis=0, keepdims=True)
                    mloc[half] = bmax if kb == 0 else jnp.maximum(mloc[half], bmax)
            for half in range(2):
                mloc_scr[half, :, cols] = mloc[half]

    def stage_b():
        for half in range(2):
            m_old = m_scr[half]
            m_new = jnp.maximum(m_old, mloc_scr[half])
            alpha_scr[half] = jnp.exp2(m_old - m_new)
            m_scr[half] = m_new
            p_scr[half, :, 0:tq] = jnp.exp2(s_scr[half, :, 0:tq] - m_new).astype(BF16)

    def stage_c(c):
        vt = vt_scr[c]
        for half in range(2):
            acc_scr[half, :, 0:tq] = (acc_scr[half, :, 0:tq] * alpha_scr[half]
                                      + _dot(vt, p_scr[half, :, 0:tq]))

    def finish_tile(qi):
        acc1, acc2 = acc_scr[0, :, 0:tq], acc_scr[1, :, 0:tq]
        o1 = acc1[0:HEAD_V] / acc1[HEAD_V:HEAD_V + 1]
        o2 = acc2[0:HEAD_V] / acc2[HEAD_V:HEAD_V + 1]
        ot = o1 - lam * o2
        ot = ot * lax.rsqrt(jnp.mean(ot * ot, axis=0, keepdims=True) + EPS)
        o_ref[tile_rows(qi), :] = (ot.T * gain).astype(BF16)

    def q_tile(qi, has_next):
        stage_b()
        stage_a(qi, 1)

        def body(c, _):
            stage_c(c)
            stage_b()
            stage_a(qi, c + 2)
            return 0

        lax.fori_loop(0, n_stages - 2, body, 0)
        stage_c(n_stages - 2)
        stage_b()
        stage_c(n_stages - 1)
        finish_tile(qi)
        if has_next:
            start_tile(qi + 1)

    n_q = (n_stages * ks) // tq
    start_tile(0)

    def q_loop(qi, _):
        q_tile(qi, True)
        return 0

    lax.fori_loop(0, n_q - 1, q_loop, 0)
    q_tile(n_q - 1, False)


def _diff_attention(proj, bias_tiles, lam_params, gain, lam_init, batch, t):
    nd = bias_tiles.shape[1]
    tile = ATT_TILE
    tq = min(ATT_QTILE, t)
    ks = min(ATT_KSTAGE, t // 2)
    n_stages = t // ks
    return pl.pallas_call(
        functools.partial(_attn_kernel, lam_init=lam_init, n_stages=n_stages, stage_keys=ks, q_tile_rows=tq),
        grid=(batch, DIFF_HEADS),
        in_specs=[
            pl.BlockSpec((t, HEAD_V), lambda b, h: (b, h)),
            pl.BlockSpec((t, HEAD_V), lambda b, h: (b, DIFF_HEADS + h)),
            pl.BlockSpec((t, HEAD_V), lambda b, h: (b, 2 * DIFF_HEADS + h)),
            pl.BlockSpec((None, nd, tile, tile), lambda b, h: (h, 0, 0, 0)),
            pl.BlockSpec(lam_params.shape, lambda b, h: (0, 0)),
            pl.BlockSpec((1, HEAD_V), lambda b, h: (0, h)),
        ],
        out_specs=pl.BlockSpec((t, HEAD_V), lambda b, h: (b, h)),
        out_shape=jax.ShapeDtypeStruct((batch * t, DIFF_HEADS * HEAD_V), BF16),
        scratch_shapes=[
            pltpu.VMEM((n_stages, ACC_ROWS, ks), BF16),
            pltpu.VMEM((2, tq, 2 * HEAD_K), BF16),
            pltpu.VMEM((2, ks, tq + LANES), F32),
            pltpu.VMEM((2, ks, tq + LANES), BF16),
            pltpu.VMEM((2, 1, tq), F32),
            pltpu.VMEM((2, 1, tq), F32),
            pltpu.VMEM((2, 1, tq), F32),
            pltpu.VMEM((2, ACC_ROWS, tq + LANES), F32),
        ],
        compiler_params=_params(("parallel", "arbitrary")),
        name="diff_attn",
    )(proj, proj, proj, bias_tiles, lam_params, gain)


def _linattn_kernel(*refs, mode, chunk, n_chunks):
    if mode == "gla":
        (q_ref, k_ref, v_ref, gate_ref, gain_ref, ga_ref, wf_ref, wb_ref, gb_ref,
         o_ref, b_scr, einc_scr, lab_scr, kvf_scr, rnext_scr) = refs
    else:
        (q_ref, k_ref, v_ref, gate_ref, gain_ref, cos_ref, sin_ref, logit_ref,
         o_ref, kx_scr, kvf_scr, rnext_scr) = refs
    c = chunk
    kw = PAIR_HEADS * HEAD_K
    vw = PAIR_HEADS * HEAD_V

    lane_k = lax.broadcasted_iota(jnp.int32, (1, kw), 1)
    head0_k = lane_k < HEAD_K
    st_row = lax.broadcasted_iota(jnp.int32, (vw, kw), 0)
    st_lane = lax.broadcasted_iota(jnp.int32, (vw, kw), 1)
    same_head = (st_row < HEAD_V) == (st_lane < HEAD_K)
    grp = max(1, LIN_BLOCK_ROWS // c)
    blk = grp * c
    ri = lax.broadcasted_iota(jnp.int32, (blk, blk), 0)
    ci = lax.broadcasted_iota(jnp.int32, (blk, blk), 1)
    same_chunk = (ri // c) == (ci // c)
    lower, upper = same_chunk & (ri >= ci), same_chunk & (ri <= ci)

    def chunk_rows(n):
        return pl.ds(pl.multiple_of(n * c, c), c)

    if mode == "ret":
        pr = lax.broadcasted_iota(jnp.int32, (kw, kw), 0)
        pc = lax.broadcasted_iota(jnp.int32, (kw, kw), 1)
        swap = (pr == (pc ^ (HEAD_K // 2))).astype(BF16)

        def rotary(x_bf16, rows):
            xr = _dot(x_bf16, swap)
            return x_bf16.astype(F32) * cos_ref[rows, :] + xr * sin_ref[rows, :]

        lg = _log_sigmoid(logit_ref[...])
        lgf, lgb = lg[0:1], lg[1:2]
        pos = lax.broadcasted_iota(jnp.int32, (c, kw), 0).astype(F32)
        b_const, e_const = (pos + 1.0) * lgf, pos * lgb
        totf_const, totb_const = float(c) * lgf, float(c) * lgb
        rel = (ri - ci).astype(F32)
        dmask = []
        for h in range(PAIR_HEADS):
            lf = lgf[:, h * HEAD_K:h * HEAD_K + 1]
            lb = lgb[:, h * HEAD_K:h * HEAD_K + 1]
            dmask.append(jnp.where(lower, jnp.exp(lf * jnp.maximum(rel, 0.0)), 0.0)
                         + jnp.where(upper, jnp.exp(lb * jnp.maximum(-rel, 0.0)), 0.0))
    else:
        ga = ga_ref[...].astype(BF16)
        b_scr[...] = _log_sigmoid(_dot(ga, wf_ref[...]) + gb_ref[0:1, :]) * (1.0 / GLA_TAU)
        lab_scr[...] = _log_sigmoid(_dot(ga, wb_ref[...]) + gb_ref[1:2, :]) * (1.0 / GLA_TAU)
        row_in_chunk = lax.broadcasted_iota(jnp.int32, (c, kw), 0)

        def chunk_cumsum(x):
            shift = 1
            while shift < c:
                x = x + jnp.where(row_in_chunk >= shift, pltpu.roll(x, shift, 0), 0.0)
                shift *= 2
            return x

    def phase1(i, r_state):
        n = n_chunks - 1 - i
        rows = chunk_rows(n)
        v = v_ref[rows, :]
        if mode == "gla":
            k = k_ref[rows, :].astype(F32)
            lab = lab_scr[rows, :]
            b, einc = chunk_cumsum(b_scr[rows, :]), chunk_cumsum(lab)
            b_scr[rows, :] = b
            einc_scr[rows, :] = einc
            e, totf, totb = einc - lab, b[c - 1:c, :], einc[c - 1:c, :]
        else:
            k = rotary(k_ref[rows, :], rows)
            kx_scr[rows, :] = k
            b, e, totf, totb = b_const, e_const, totf_const, totb_const
        kf = (k * jnp.exp(totf - b)).astype(BF16)
        kb = (k * jnp.exp(e)).astype(BF16)
        kvf = lax.dot_general(v, kf, _TN, preferred_element_type=F32)
        kvb = lax.dot_general(v, kb, _TN, preferred_element_type=F32)
        kvf_scr[n] = jnp.where(same_head, kvf, 0.0)
        rnext_scr[n] = r_state.astype(BF16)
        return jnp.exp(totb) * r_state + jnp.where(same_head, kvb, 0.0)

    unroll = min(n_chunks, LIN_UNROLL)
    lax.fori_loop(0, n_chunks, phase1, jnp.zeros((vw, kw), F32), unroll=unroll)

    def rows_of_chunks(values):
        return jnp.concatenate([jnp.broadcast_to(x, (c, kw)) for x in values], axis=0)

    def phase2(nb, s_state):
        rows = pl.ds(pl.multiple_of(nb * blk, blk), blk)
        v = v_ref[rows, :]
        if mode == "gla":
            q = q_ref[rows, :].astype(F32)
            k = k_ref[rows, :].astype(F32)
            b, einc = b_scr[rows, :], einc_scr[rows, :]
            e = einc - lab_scr[rows, :]
            totf = [b[g * c + c - 1:(g + 1) * c, :] for g in range(grp)]
            totb = [einc[g * c + c - 1:(g + 1) * c, :] for g in range(grp)]
        else:
            q = rotary(q_ref[rows, :], rows)
            k = kx_scr[rows, :]
            b, e, totf, totb = b_const, e_const, [totf_const], [totb_const]

        qf = (q * jnp.exp(b)).astype(BF16)
        qb = (q * jnp.exp(rows_of_chunks(totb) - e)).astype(BF16)
        inter = []
        for g in range(grp):
            n = nb * grp + g
            crows = slice(g * c, (g + 1) * c)
            inter.append(lax.dot_general(qf[crows], s_state.astype(BF16), _NT, preferred_element_type=F32)
                         + lax.dot_general(qb[crows], rnext_scr[n], _NT, preferred_element_type=F32))
            s_state = jnp.exp(totf[g]) * s_state + kvf_scr[n]
        inter = jnp.concatenate(inter, axis=0)

        if mode == "gla":
            bm = rows_of_chunks([b[g * c + c // 2 - 1:g * c + c // 2, :] for g in range(grp)])
            em = rows_of_chunks([e[g * c + c // 2:g * c + c // 2 + 1, :] for g in range(grp)])
            qtf, ktf = q * jnp.exp(b - bm), (k * jnp.exp(bm - b)).astype(BF16)
            qtb, ktb = q * jnp.exp(em - e), (k * jnp.exp(e - em)).astype(BF16)
        else:
            kb16 = k.astype(BF16)
        outs = []
        for h in range(PAIR_HEADS):
            mh = head0_k if h == 0 else jnp.logical_not(head0_k)
            if mode == "gla":
                sf = lax.dot_general(jnp.where(mh, qtf, 0.0).astype(BF16), ktf, _NT, preferred_element_type=F32)
                sb = lax.dot_general(jnp.where(mh, qtb, 0.0).astype(BF16), ktb, _NT, preferred_element_type=F32)
                sc = jnp.where(lower, sf, 0.0) + jnp.where(upper, sb, 0.0)
            else:
                sc = lax.dot_general(jnp.where(mh, q, 0.0).astype(BF16), kb16, _NT,
                                     preferred_element_type=F32) * dmask[h]
            outs.append(_dot(sc.astype(BF16), v[:, h * HEAD_V:(h + 1) * HEAD_V]))

        gate = gate_ref[rows, :].astype(F32)
        for h in range(PAIR_HEADS):
            cols = slice(h * HEAD_V, (h + 1) * HEAD_V)
            o = outs[h] + inter[:, cols]
            o = _rms_rows(o, gain_ref[:, cols]) * _silu(gate[:, cols])
            o_ref[rows, cols] = o.astype(BF16)
        return s_state

    n_blocks = n_chunks // grp
    lax.fori_loop(0, n_blocks, phase2, jnp.zeros((vw, kw), F32), unroll=min(n_blocks, max(1, LIN_UNROLL // grp)))


def _linattn(mode, proj, col0, gain, extra, batch, t):
    chunk = GLA_CHUNK if mode == "gla" else RET_CHUNK
    n_chunks = t // chunk
    kw, vw = PAIR_HEADS * HEAD_K, PAIR_HEADS * HEAD_V
    v0 = (col0 + 2 * N_PAIRS) // 2
    g0 = v0 + N_PAIRS
    in_specs = [
        pl.BlockSpec((t, kw), lambda b, p: (b, col0 + p)),
        pl.BlockSpec((t, kw), lambda b, p: (b, col0 + N_PAIRS + p)),
        pl.BlockSpec((t, vw), lambda b, p: (b, v0 + p)),
        pl.BlockSpec((t, vw), lambda b, p: (b, g0 + p)),
        pl.BlockSpec((1, vw), lambda b, p: (0, p)),
    ]
    args = [proj, proj, proj, proj, gain]
    state_scratch = [pltpu.VMEM((n_chunks, vw, kw), F32), pltpu.VMEM((n_chunks, vw, kw), BF16)]
    if mode == "gla":
        ga, wf, wb, gb = extra
        in_specs += [
            pl.BlockSpec((t, LANES), lambda b, p: (b, 0)),
            pl.BlockSpec((LANES, kw), lambda b, p: (0, p)),
            pl.BlockSpec((LANES, kw), lambda b, p: (0, p)),
            pl.BlockSpec((2, kw), lambda b, p: (0, p)),
        ]
        args += [ga, wf, wb, gb]
        scratch = [pltpu.VMEM((t, kw), F32)] * 3 + state_scratch
    else:
        cos, sin, logit = extra
        in_specs += [
            pl.BlockSpec((t, kw), lambda b, p: (0, 0)),
            pl.BlockSpec((t, kw), lambda b, p: (0, 0)),
            pl.BlockSpec((2, kw), lambda b, p: (0, p)),
        ]
        args += [cos, sin, logit]
        scratch = [pltpu.VMEM((t, kw), F32)] + state_scratch
    return pl.pallas_call(
        functools.partial(_linattn_kernel, mode=mode, chunk=chunk, n_chunks=n_chunks),
        grid=(batch, N_PAIRS),
        in_specs=in_specs,
        out_specs=pl.BlockSpec((t, vw), lambda b, p: (b, p)),
        out_shape=jax.ShapeDtypeStruct((batch * t, N_PAIRS * vw), BF16),
        scratch_shapes=scratch,
        compiler_params=_params(("parallel", "arbitrary")),
        name="linattn_" + mode,
    )(*args)


def _outproj_kernel(x_ref, a_ref, b_ref, c_ref, wa_ref, wb_ref, wc_ref, o_ref):
    o_ref[...] = (x_ref[...] + _dot(a_ref[...], wa_ref[...].astype(BF16))
                  + _dot(b_ref[...], wb_ref[...].astype(BF16)) + _dot(c_ref[...], wc_ref[...].astype(BF16)))


def _outproj(x2, a, b, c, w_o, layer):
    m, d = x2.shape
    tm, tn = min(MM_TILE_M // 2, m), d
    wa_rows, wb_rows = a.shape[1], b.shape[1]
    nb = wa_rows // wb_rows
    return pl.pallas_call(
        _outproj_kernel,
        grid=(m // tm, d // tn),
        in_specs=[
            pl.BlockSpec((tm, tn), lambda i, j: (i, j)),
            pl.BlockSpec((tm, wa_rows), lambda i, j: (i, 0)),
            pl.BlockSpec((tm, wb_rows), lambda i, j: (i, 0)),
            pl.BlockSpec((tm, wb_rows), lambda i, j: (i, 0)),
            pl.BlockSpec((None, wa_rows, tn), lambda i, j: (layer, 0, j), pipeline_mode=pl.Buffered(1)),
            pl.BlockSpec((None, wb_rows, tn), lambda i, j: (layer, nb, j), pipeline_mode=pl.Buffered(1)),
            pl.BlockSpec((None, wb_rows, tn), lambda i, j: (layer, nb + 1, j), pipeline_mode=pl.Buffered(1)),
        ],
        out_specs=pl.BlockSpec((tm, tn), lambda i, j: (i, j)),
        out_shape=jax.ShapeDtypeStruct((m, d), F32),
        compiler_params=_params(("parallel", "arbitrary")),
        name="outproj",
    )(x2, a, b, c, w_o, w_o, w_o)


def _ffn_up_kernel(x_ref, xp_ref, xn_ref, g_ref, wa_ref, wg_ref, cwa_ref, cwg_ref, cba_ref, cbg_ref,
                   o_ref, h_scr, *, tiles_per_seq):
    i = pl.program_id(0)
    tm = x_ref.shape[0]

    @pl.when(pl.program_id(1) == 0)
    def _():
        g = g_ref[...]
        keep_prev = (i % tiles_per_seq != 0).astype(F32)
        keep_next = ((i + 1) % tiles_per_seq != 0).astype(F32)
        h_scr[0:HALO, :] = (_rms_rows(xp_ref[...], g) * keep_prev).astype(BF16)
        h_scr[HALO:HALO + tm, :] = _rms_rows(x_ref[...], g).astype(BF16)
        h_scr[HALO + tm:, :] = (_rms_rows(xn_ref[...], g) * keep_next).astype(BF16)

    h = h_scr[...]
    n_ext = tm + 2 * HALO

    def conv(w_ref, cw_ref, cb_ref):
        u = _dot(h, w_ref[...].astype(BF16))
        prev = pltpu.roll(u, 1, 0)[HALO:HALO + tm]
        nxt = pltpu.roll(u, n_ext - 1, 0)[HALO:HALO + tm]
        return prev * cw_ref[0:1, :] + u[HALO:HALO + tm] * cw_ref[1:2, :] + nxt * cw_ref[2:3, :] + cb_ref[...]

    gate = _silu(conv(wg_ref, cwg_ref, cbg_ref))
    o_ref[...] = (conv(wa_ref, cwa_ref, cba_ref) * gate).astype(BF16)


def _ffn_up(x2, g, w_up, conv_w, conv_b, t, layer):
    m, d = x2.shape
    d_ff = w_up.shape[-1] // 2
    tm = min(MM_TILE_M, t)
    tn = FFN_TILE_N
    nj = d_ff // tn
    blocks_per_tile = tm // HALO
    last_block = m // HALO - 1
    return pl.pallas_call(
        functools.partial(_ffn_up_kernel, tiles_per_seq=t // tm),
        grid=(m // tm, nj),
        in_specs=[
            pl.BlockSpec((tm, d), lambda i, j: (i, 0)),
            pl.BlockSpec((HALO, d), lambda i, j: (jnp.maximum(i * blocks_per_tile - 1, 0), 0)),
            pl.BlockSpec((HALO, d), lambda i, j: (jnp.minimum((i + 1) * blocks_per_tile, last_block), 0)),
            pl.BlockSpec((1, d), lambda i, j: (0, 0)),
            pl.BlockSpec((None, d, tn), lambda i, j: (layer, 0, j)),
            pl.BlockSpec((None, d, tn), lambda i, j: (layer, 0, nj + j)),
            pl.BlockSpec((CONV_W, tn), lambda i, j: (0, j)),
            pl.BlockSpec((CONV_W, tn), lambda i, j: (0, nj + j)),
            pl.BlockSpec((1, tn), lambda i, j: (0, j)),
            pl.BlockSpec((1, tn), lambda i, j: (0, nj + j)),
        ],
        out_specs=pl.BlockSpec((tm, tn), lambda i, j: (i, j)),
        out_shape=jax.ShapeDtypeStruct((m, d_ff), BF16),
        scratch_shapes=[pltpu.VMEM((tm + 2 * HALO, d), BF16)],
        compiler_params=_params(("parallel", "arbitrary")),
        name="ffn_up",
    )(x2, x2, x2, g, w_up, w_up, conv_w, conv_w, conv_b, conv_b)


def _ffn_down_kernel(x_ref, a_ref, w_ref, g_ref, o_ref, *, final_norm):
    y = x_ref[...] + _dot(a_ref[...], w_ref[...])
    o_ref[...] = _rms_rows(y, g_ref[...]) if final_norm else y


def _ffn_down(x2, act, w_down, final_g, layer, final_norm):
    m, d = x2.shape
    k = act.shape[1]
    tm = min(FFN_DOWN_TILE_M, m)
    return pl.pallas_call(
        functools.partial(_ffn_down_kernel, final_norm=final_norm),
        grid=(m // tm,),
        in_specs=[
            pl.BlockSpec((tm, d), lambda i: (i, 0)),
            pl.BlockSpec((tm, k), lambda i: (i, 0)),
            pl.BlockSpec((None, k, d), lambda i: (layer, 0, 0), pipeline_mode=pl.Buffered(1)),
            pl.BlockSpec((1, d), lambda i: (0, 0)),
        ],
        out_specs=pl.BlockSpec((tm, d), lambda i: (i, 0)),
        out_shape=jax.ShapeDtypeStruct((m, d), F32),
        compiler_params=_params(("parallel",)),
        name="ffn_down",
    )(x2, act, w_down, final_g)


def _rotary_tables(t):
    half = HEAD_K // 2
    inv = 1.0 / (ROPE_BASE ** jnp.linspace(0.0, 1.0, half, dtype=F32))
    ang = jnp.arange(t, dtype=jnp.int32).astype(F32)[:, None] * inv[None, :]
    cos, sin = jnp.cos(ang), jnp.sin(ang)
    cos_l = jnp.tile(cos, (1, 2 * PAIR_HEADS))
    sin_l = jnp.tile(jnp.concatenate([-sin, sin], axis=-1), (1, PAIR_HEADS))
    return cos_l, sin_l


def kernel(x, ln1_g, w_in, diff_lambda, rel_bias, gla_gate_w, gla_gate_b, ret_decay_logit, head_gain,
           w_o, ln2_g, w_up, conv_w, conv_b, w_down, final_g):
    batch, t, d = x.shape
    depth = w_in.shape[0]
    diff_w = DIFF_HEADS * HEAD_V
    gla_w = 2 * N_PAIRS * HEAD_V
    gla_k = 2 * N_PAIRS * HEAD_K
    ga0 = 3 * diff_w + 2 * gla_k + 2 * gla_w
    ga1 = ga0 + 2 * GLA_GATE_RANK
    gla_col0 = 3 * diff_w // LANES
    ret_col0 = ga0 // LANES

    w_main = jnp.concatenate([w_in[:, :, :ga0], w_in[:, :, ga1:]], axis=-1).astype(BF16)
    w_ga = jnp.pad(w_in[:, :, ga0:ga1], ((0, 0), (0, 0), (0, LANES - 2 * GLA_GATE_RANK))).astype(BF16)
    w_down16 = w_down.astype(BF16)
    k_scale = HEAD_K ** -0.5
    colscale = jnp.ones((1, w_main.shape[-1]), F32)
    colscale = colscale.at[:, :diff_w].set(k_scale * LOG2E)
    colscale = colscale.at[:, 3 * diff_w:3 * diff_w + gla_k].set(k_scale)
    colscale = colscale.at[:, ga0 + gla_k:ga0 + 2 * gla_k].set(k_scale)
    r = GLA_GATE_RANK
    gate_wf = jnp.pad(gla_gate_w[:, 0], ((0, 0), (0, LANES - r), (0, 0))).astype(BF16)
    gate_wb = jnp.pad(gla_gate_w[:, 1], ((0, 0), (r, LANES - 2 * r), (0, 0))).astype(BF16)
    ret_logit = jnp.repeat(ret_decay_logit, HEAD_K, axis=-1)

    rel = (jnp.arange(ATT_TILE, dtype=jnp.int32)[:, None] - jnp.arange(ATT_TILE, dtype=jnp.int32)[None, :])
    tile_offsets = jnp.arange(2 * BIAS_REACH + 1, dtype=jnp.int32) - BIAS_REACH
    rel = rel[None] + ATT_TILE * tile_offsets[:, None, None]
    bias_tiles = _bias_tiles(rel_bias.astype(F32), _t5_bucket(rel))
    cos_l, sin_l = _rotary_tables(t)

    x2 = x.reshape(batch * t, d)
    for layer in range(depth):
        lam_init = 0.8 - 0.6 * math.exp(-0.3 * layer)
        gain = head_gain[layer][None, :]
        proj, ga = _inproj(x2, ln1_g[layer][None, :], w_main, w_ga, colscale, layer)
        a_out = _diff_attention(proj, bias_tiles, diff_lambda[layer], gain[:, :diff_w], lam_init, batch, t)
        b_out = _linattn("gla", proj, gla_col0, gain[:, diff_w:diff_w + gla_w],
                         (ga, gate_wf[layer], gate_wb[layer], gla_gate_b[layer]), batch, t)
        c_out = _linattn("ret", proj, ret_col0, gain[:, diff_w + gla_w:],
                         (cos_l, sin_l, ret_logit[layer]), batch, t)
        x2 = _outproj(x2, a_out, b_out, c_out, w_o, layer)
        act = _ffn_up(x2, ln2_g[layer][None, :], w_up, conv_w[layer], conv_b[layer][None, :], t, layer)
        x2 = _ffn_down(x2, act, w_down16, final_g[None, :], layer, final_norm=(layer == depth - 1))
    return x2.reshape(batch, t, d)
```
